```python
import jax
import jax.numpy as jnp
from jax import lax
import numpy as np

D_MODEL = 1024
BATCH = 8
SEQ = 4096
DEPTH = 2

HEAD_DIM = 64
BRANCH_WIDTH = 256
N_BRANCH = 4
HG_HEADS = 4
HG_CHUNK = 16
LB_FLOOR = 1e-30
RET_HEADS = 4
RET_CHUNK = 64
ATT_HEADS = 4
ATT_KV_HEADS = 2
ATT_GROUP = ATT_HEADS // ATT_KV_HEADS
WINDOW = 128
ATT_BLOCK = 128
MASK_VALUE = -1e30
LRU_WIDTH = 256
LRU_BLOCKS = 4
LRU_BLOCK_DIM = LRU_WIDTH // LRU_BLOCKS
CONV_WIDTH = 4
LRU_C = 8.0
D_FF = 2816
EPS = 1e-6

IN_SIZES = (
    BRANCH_WIDTH, BRANCH_WIDTH, BRANCH_WIDTH, BRANCH_WIDTH,
    BRANCH_WIDTH, BRANCH_WIDTH, BRANCH_WIDTH, BRANCH_WIDTH,
    ATT_HEADS * HEAD_DIM, ATT_KV_HEADS * HEAD_DIM, ATT_KV_HEADS * HEAD_DIM,
    LRU_WIDTH, LRU_WIDTH,
    N_BRANCH * D_MODEL,
)
D_IN = sum(IN_SIZES)

kernel_name = "hybrid_hgrn2_retnet_swa_rglru_macaron"


def _rmsnorm(x, w):
    xf = x.astype(jnp.float32)
    y = xf * lax.rsqrt(jnp.mean(xf * xf, axis=-1, keepdims=True) + EPS)
    return (y * w.astype(jnp.float32)).astype(x.dtype)


def _swiglu(x, wg, wu, wd):
    return (jax.nn.silu(x @ wg) * (x @ wu)) @ wd


def _split_cols(t, sizes):
    out, start = [], 0
    for s in sizes:
        out.append(t[..., start:start + s])
        start += s
    return out


def _to_chunks(t, n_heads, chunk):
    B, T, _ = t.shape
    return t.reshape(B, T // chunk, chunk, n_heads, -1).transpose(0, 3, 1, 2, 4)


def _from_chunks(o):
    B, H, N, C, d = o.shape
    return o.transpose(0, 2, 3, 1, 4).reshape(B, N * C, H * d)


def _chunk_state_scan(decay, dS):
    def step(S, inp):
        dec, ds = inp
        return dec[..., None] * S + ds, S
    S0 = jnp.zeros_like(dS[:, :, 0])
    _, S_start = lax.scan(step, S0, (jnp.moveaxis(decay, 2, 0), jnp.moveaxis(dS, 2, 0)))
    return jnp.moveaxis(S_start, 0, 2)


def _hgrn2(q, f_pre, i, g, lb, out_w):
    B, T, _ = q.shape
    H, d, C = HG_HEADS, HEAD_DIM, HG_CHUNK
    dtype = q.dtype
    z = f_pre.astype(jnp.float32)
    lb = lb.astype(jnp.float32)
    log_f = jnp.logaddexp(jnp.log(jnp.maximum(lb, LB_FLOOR)), jnp.log1p(-lb) + jax.nn.log_sigmoid(z))
    k = (1.0 - lb) * jax.nn.sigmoid(-z)
    qc = _to_chunks(q.astype(jnp.float32), H, C)
    kc = _to_chunks(k, H, C)
    vc = _to_chunks(i.astype(jnp.float32), H, C)
    b = jnp.cumsum(_to_chunks(log_f, H, C), axis=3)
    b_ref = b[:, :, :, C // 2:C // 2 + 1]
    qt = qc * jnp.exp(b - b_ref)
    kt = kc * jnp.exp(b_ref - b)
    causal = jnp.tril(jnp.ones((C, C), dtype=bool))
    att = jnp.where(causal, jnp.einsum('bhntk,bhnsk->bhnts', qt, kt), 0.0)
    o = jnp.einsum('bhnts,bhnsv->bhntv', att, vc)
    b_last = b[:, :, :, -1:]
    dS = jnp.einsum('bhnsk,bhnsv->bhnkv', kc * jnp.exp(b_last - b), vc)
    S_start = _chunk_state_scan(jnp.exp(b_last[:, :, :, 0]), dS)
    o = o + jnp.einsum('bhntk,bhnkv->bhntv', qc * jnp.exp(b), S_start)
    o = _from_chunks(o).reshape(B, T, H, d)
    o = _rmsnorm(o, out_w.reshape(H, d)) * jax.nn.sigmoid(g.astype(jnp.float32)).reshape(B, T, H, d)
    return o.reshape(B, T, H * d).astype(dtype)


def _retention(q, k, v, g, gn_w, gn_b):
    B, T, _ = q.shape
    H, d, C = RET_HEADS, HEAD_DIM, RET_CHUNK
    N = T // C
    dtype = q.dtype
    log_gamma = jnp.log1p(-jnp.exp2(-5.0 - jnp.arange(H, dtype=jnp.float32)))
    qc = _to_chunks(q.astype(jnp.float32), H, C) * d ** -0.5
    kc = _to_chunks(k.astype(jnp.float32), H, C)
    vc = _to_chunks(v.astype(jnp.float32), H, C)
    pos = jnp.arange(C, dtype=jnp.float32)
    rel = pos[:, None] - pos[None, :]
    decay = jnp.where(rel >= 0, jnp.exp(log_gamma[:, None, None] * jnp.maximum(rel, 0.0)), 0.0)
    att = jnp.einsum('bhntk,bhnsk->bhnts', qc, kc) * decay[None, :, None]
    o = jnp.einsum('bhnts,bhnsv->bhntv', att, vc)
    lg = log_gamma[None, :, None, None, None]
    dS = jnp.einsum('bhnsk,bhnsv->bhnkv', kc * jnp.exp(lg * (C - 1.0 - pos)[:, None]), vc)
    chunk_decay = jnp.broadcast_to(jnp.exp(log_gamma * C)[None, :, None, None], (B, H, N, d))
    S_start = _chunk_state_scan(chunk_decay, dS)
    o = o + jnp.einsum('bhntk,bhnkv->bhntv', qc * jnp.exp(lg * (pos + 1.0)[:, None]), S_start)
    o = _from_chunks(o).reshape(B, T, H, d)
    mu = jnp.mean(o, axis=-1, keepdims=True)
    var = jnp.mean(jnp.square(o - mu), axis=-1, keepdims=True)
    o = (o - mu) * lax.rsqrt(var + EPS) * gn_w.astype(jnp.float32).reshape(H, d) + gn_b.astype(jnp.float32).reshape(H, d)
    o = o * jax.nn.silu(g.astype(jnp.float32)).reshape(B, T, H, d)
    return o.reshape(B, T, H * d).astype(dtype)


def _swa(q, k, v, qn_w, kn_w, sinks):
    B, T, _ = q.shape
    Q, KV, G, d = ATT_BLOCK, ATT_KV_HEADS, ATT_GROUP, HEAD_DIM
    N = T // Q
    dtype = q.dtype
    q = _rmsnorm(q.reshape(B, T, KV, G, d), qn_w).astype(jnp.float32)
    k = _rmsnorm(k.reshape(B, T, KV, d), kn_w).astype(jnp.float32)
    v = v.reshape(B, T, KV, d).astype(jnp.float32)
    qb = q.reshape(B, N, Q, KV, G, d)

    def with_prev(t):
        tb = t.reshape(B, N, Q, KV, d)
        prev = jnp.concatenate([jnp.zeros_like(tb[:, :1]), tb[:, :-1]], axis=1)
        return jnp.concatenate([prev, tb], axis=2)

    kw, vw = with_prev(k), with_prev(v)
    s = jnp.einsum('bnqkgd,bnskd->bnkgqs', qb, kw) * d ** -0.5
    dist = jnp.arange(Q)[:, None] + Q - jnp.arange(2 * Q)[None, :]
    valid = (dist >= 0) & (dist < WINDOW)
    valid = valid[None] & ((jnp.arange(N)[:, None, None] > 0) | (jnp.arange(2 * Q)[None, None, :] >= Q))
    slopes = jnp.exp2(-8.0 * jnp.arange(1, ATT_HEADS + 1, dtype=jnp.float32) / ATT_HEADS).reshape(KV, G, 1, 1)
    s = s - slopes * dist.astype(jnp.float32)
    s = jnp.where(valid[None, :, None, None], s, MASK_VALUE)
    sink = jnp.broadcast_to(sinks.astype(jnp.float32).reshape(1, 1, KV, G, 1, 1), s.shape[:-1] + (1,))
    p = jax.nn.softmax(jnp.concatenate([s, sink], axis=-1), axis=-1)[..., :-1]
    o = jnp.einsum('bnkgqs,bnskd->bnqkgd', p, vw)
    return o.reshape(B, T, KV * G * d).astype(dtype)


def _rglru(xb, gb, conv_w, conv_b, wa, ba, wx, bx, lam):
    B, T, W = xb.shape
    dtype = xb.dtype
    xf = xb.astype(jnp.float32)
    xp = jnp.pad(xf, ((0, 0), (CONV_WIDTH - 1, 0), (0, 0)))
    xc = conv_b.astype(jnp.float32)
    for j in range(CONV_WIDTH):
        xc = xc + xp[:, j:j + T] * conv_w[j].astype(jnp.float32)
    xr = xc.reshape(B, T, LRU_BLOCKS, LRU_BLOCK_DIM)
    r = jax.nn.sigmoid(jnp.einsum('btnc,nce->btne', xr, wa.astype(jnp.float32)).reshape(B, T, W) + ba.astype(jnp.float32))
    i = jax.nn.sigmoid(jnp.einsum('btnc,nce->btne', xr, wx.astype(jnp.float32)).reshape(B, T, W) + bx.astype(jnp.float32))
    log_a = -LRU_C * r * jax.nn.softplus(-lam.astype(jnp.float32))
    a = jnp.exp(log_a)
    u = jnp.sqrt(-jnp.expm1(2.0 * log_a)) * (i * xc)

    def combine(left, right):
        a_l, b_l = left
        a_r, b_r = right
        return a_l * a_r, a_r * b_l + b_r

    _, h = lax.associative_scan(combine, (a, u), axis=1)
    y = h * jax.nn.gelu(gb.astype(jnp.float32))
    return y.astype(dtype)


def setup_inputs(seed: int = 0) -> dict:
    key = jax.random.key(seed)
    kit = iter(jax.random.split(key, 40))
    L, D, F, W, bd = DEPTH, D_MODEL, D_FF, BRANCH_WIDTH, LRU_BLOCK_DIM

    def normal(shape, scale):
        return scale * jax.random.normal(next(kit), shape, jnp.float32)

    def gain(shape):
        return 1.0 + normal(shape, 0.02)

    x = normal((BATCH, SEQ, D), 1.0)
    ffn1_norm = gain((L, D))
    ffn1_wg = normal((L, D, F), D ** -0.5)
    ffn1_wu = normal((L, D, F), D ** -0.5)
    ffn1_wd = normal((L, F, D), F ** -0.5)
    mix_norm = gain((L, D))
    w_in = normal((L, D, D_IN), D ** -0.5)
    gate_bias = normal((L, N_BRANCH, D), 0.02)
    hgrn_lb_logits = normal((L, W), 0.5)
    hgrn_out_norm = gain((L, W))
    ret_gn_w = gain((L, W))
    ret_gn_b = normal((L, W), 0.02)
    attn_q_norm = gain((L, HEAD_DIM))
    attn_k_norm = gain((L, HEAD_DIM))
    attn_sinks = normal((L, ATT_HEADS), 1.0)
    lru_conv_w = normal((L, CONV_WIDTH, LRU_WIDTH), CONV_WIDTH ** -0.5)
    lru_conv_b = normal((L, LRU_WIDTH), 0.02)
    lru_wa = normal((L, LRU_BLOCKS, bd, bd), bd ** -0.5)
    lru_ba = normal((L, LRU_WIDTH), 0.02)
    lru_wx = normal((L, LRU_BLOCKS, bd, bd), bd ** -0.5)
    lru_bx = normal((L, LRU_WIDTH), 0.02)
    u = jax.random.uniform(next(kit), (L, LRU_WIDTH), jnp.float32, 0.9, 0.999)
    a0 = u ** (1.0 / LRU_C)
    lru_lambda = jnp.log(a0) - jnp.log1p(-a0)
    w_branch = normal((L, N_BRANCH, W, D), W ** -0.5)
    w_out = normal((L, D, D), D ** -0.5)
    ffn2_norm = gain((L, D))
    ffn2_wg = normal((L, D, F), D ** -0.5)
    ffn2_wu = normal((L, D, F), D ** -0.5)
    ffn2_wd = normal((L, F, D), F ** -0.5)
    return {
        "x": x, "ffn1_norm": ffn1_norm, "ffn1_wg": ffn1_wg, "ffn1_wu": ffn1_wu, "ffn1_wd": ffn1_wd,
        "mix_norm": mix_norm, "w_in": w_in, "gate_bias": gate_bias,
        "hgrn_lb_logits": hgrn_lb_logits, "hgrn_out_norm": hgrn_out_norm,
        "ret_gn_w": ret_gn_w, "ret_gn_b": ret_gn_b,
        "attn_q_norm": attn_q_norm, "attn_k_norm": attn_k_norm, "attn_sinks": attn_sinks,
        "lru_conv_w": lru_conv_w, "lru_conv_b": lru_conv_b, "lru_wa": lru_wa, "lru_ba": lru_ba,
        "lru_wx": lru_wx, "lru_bx": lru_bx, "lru_lambda": lru_lambda,
        "w_branch": w_branch, "w_out": w_out,
        "ffn2_norm": ffn2_norm, "ffn2_wg": ffn2_wg, "ffn2_wu": ffn2_wu, "ffn2_wd": ffn2_wd,
    }


def reference(x, ffn1_norm, ffn1_wg, ffn1_wu, ffn1_wd, mix_norm, w_in, gate_bias,
              hgrn_lb_logits, hgrn_out_norm, ret_gn_w, ret_gn_b,
              attn_q_norm, attn_k_norm, attn_sinks,
              lru_conv_w, lru_conv_b, lru_wa, lru_ba, lru_wx, lru_bx, lru_lambda,
              w_branch, w_out, ffn2_norm, ffn2_wg, ffn2_wu, ffn2_wd):
    B, T, D = x.shape
    lb_p = jax.nn.softmax(hgrn_lb_logits.astype(jnp.float32), axis=0)
    lower_bounds = jnp.cumsum(lb_p, axis=0) - lb_p[0:1]
    for l in range(DEPTH):
        x = x + 0.5 * _swiglu(_rmsnorm(x, ffn1_norm[l]), ffn1_wg[l], ffn1_wu[l], ffn1_wd[l])
        h = _rmsnorm(x, mix_norm[l])
        proj = h @ w_in[l]
        (hq, hf, hi, hg, rq, rk, rv, rg, aq, ak, av, lx, lg, gate_pre) = _split_cols(proj, IN_SIZES)
        y_a = _hgrn2(hq, hf, hi, hg, lower_bounds[l], hgrn_out_norm[l])
        y_b = _retention(rq, rk, rv, rg, ret_gn_w[l], ret_gn_b[l])
        y_c = _swa(aq, ak, av, attn_q_norm[l], attn_k_norm[l], attn_sinks[l])
        y_d = _rglru(lx, lg, lru_conv_w[l], lru_conv_b[l], lru_wa[l], lru_ba[l],
                     lru_wx[l], lru_bx[l], lru_lambda[l])
        gates = jax.nn.sigmoid(gate_pre.astype(jnp.float32).reshape(B, T, N_BRANCH, D)
                               + gate_bias[l].astype(jnp.float32)).astype(x.dtype)
        merged = gates[:, :, 0] * (y_a @ w_branch[l, 0])
        merged = merged + gates[:, :, 1] * (y_b @ w_branch[l, 1])
        merged = merged + gates[:, :, 2] * (y_c @ w_branch[l, 2])
        merged = merged + gates[:, :, 3] * (y_d @ w_branch[l, 3])
        x = x + merged @ w_out[l]
        x = x + 0.5 * _swiglu(_rmsnorm(x, ffn2_norm[l]), ffn2_wg[l], ffn2_wu[l], ffn2_wd[l])
    return x
```

```python
import functools
import math

import jax
import jax.numpy as jnp
from jax import lax
from jax.experimental import pallas as pl
from jax.experimental.pallas import tpu as pltpu

HEAD_DIM = 64
BRANCH_WIDTH = 256
N_BRANCH = 4
N_HEADS = 4
HG_CHUNK = 16
LB_FLOOR = 1e-30
ATT_KV_HEADS = 2
ATT_BLOCK = 128
WINDOW = 128
MASK_VALUE = -1e30
CONV_WIDTH = 4
LRU_C = 8.0
EPS = 1e-6

V7X_VMEM_BYTES = 64 * 1024 * 1024
VMEM_LIMIT_BYTES = V7X_VMEM_BYTES - 8 * 1024 * 1024
SUBLANES = 8

FFN_ROWS = 512
FFN_COLS = 256
MIX_ROWS = 256

BF16 = jnp.bfloat16
F32 = jnp.float32


def _dot(a, b):
    return jnp.dot(a.astype(BF16), b.astype(BF16), preferred_element_type=F32)


def _dot_nt(a, b):
    return lax.dot_general(a.astype(BF16), b.astype(BF16), (((1,), (1,)), ((), ())),
                           preferred_element_type=F32)


def _dot_tn(a, b):
    return lax.dot_general(a.astype(BF16), b.astype(BF16), (((0,), (0,)), ((), ())),
                           preferred_element_type=F32)


def _dot_split(a, m):
    hi = a.astype(BF16)
    lo = (a - hi.astype(F32)).astype(BF16)
    return (jnp.dot(hi, m, preferred_element_type=F32) + jnp.dot(lo, m, preferred_element_type=F32))


def _rmsnorm_rows(x, w_row):
    return (x * lax.rsqrt(jnp.mean(x * x, axis=-1, keepdims=True) + EPS)) * w_row


def _softplus(x):
    return jnp.maximum(x, 0.0) + jnp.log1p(jnp.exp(-jnp.abs(x)))


def _expm1_nonpositive(x):
    u = jnp.exp(x)
    stable = (u - 1.0) * x / jnp.log(jnp.where(u < 1.0, u, 0.5))
    return jnp.where(u == 1.0, x, jnp.where(x < -80.0, -1.0, stable))


def _iota(shape, axis):
    return lax.broadcasted_iota(jnp.int32, shape, axis)


def _ffn_body(x_ref, nw_ref, wg_ref, wu_ref, wd_ref, o_ref):
    x = x_ref[...]
    h = _rmsnorm_rows(x, nw_ref[...]).astype(BF16)
    d_ff = wg_ref.shape[1]
    acc = jnp.zeros(x.shape, F32)
    for c in range(d_ff // FFN_COLS):
        cols = slice(c * FFN_COLS, (c + 1) * FFN_COLS)
        g = jnp.dot(h, wg_ref[:, cols], preferred_element_type=F32)
        u = jnp.dot(h, wu_ref[:, cols], preferred_element_type=F32)
        a = (jax.nn.silu(g) * u).astype(BF16)
        acc = acc + jnp.dot(a, wd_ref[cols, :], preferred_element_type=F32)
    o_ref[...] = x + 0.5 * acc


def _const_spec(shape):
    zeros = (0,) * len(shape)
    return pl.BlockSpec(shape, lambda *_: zeros, pipeline_mode=pl.Buffered(1))


def _ffn(x2, norm_w, wg, wu, wd):
    m, d = x2.shape
    d_ff = wg.shape[1]
    assert m % FFN_ROWS == 0 and d_ff % FFN_COLS == 0
    return pl.pallas_call(
        _ffn_body,
        grid=(m // FFN_ROWS,),
        in_specs=[
            pl.BlockSpec((FFN_ROWS, d), lambda i: (i, 0)),
            _const_spec((1, d)),
            _const_spec((d, d_ff)),
            _const_spec((d, d_ff)),
            _const_spec((d_ff, d)),
        ],
        out_specs=pl.BlockSpec((FFN_ROWS, d), lambda i: (i, 0)),
        out_shape=jax.ShapeDtypeStruct((m, d), F32),
        compiler_params=pltpu.CompilerParams(
            dimension_semantics=("arbitrary",), vmem_limit_bytes=VMEM_LIMIT_BYTES),
        name="channel_mixer",
    )(x2, norm_w.reshape(1, d), wg, wu, wd)


def _head_mean_matrix(width):
    r = _iota((width, width), 0) // HEAD_DIM
    c = _iota((width, width), 1) // HEAD_DIM
    return jnp.where(r == c, 1.0 / HEAD_DIM, 0.0).astype(BF16)


def _hgrn2_block(q, z, iv, g, lb, out_w, st_ref):
    rows, width = q.shape
    n_chunks = rows // HG_CHUNK
    log_lb = jnp.log(jnp.maximum(lb, LB_FLOOR))
    log_sig = -_softplus(-z)
    c = jnp.log1p(-lb) + log_sig
    log_f = jnp.maximum(log_lb, c) + jnp.log1p(jnp.exp(-jnp.abs(log_lb - c)))
    k = (1.0 - lb) * jax.nn.sigmoid(-z)

    r_i = _iota((ATT_BLOCK, ATT_BLOCK), 0)
    c_i = _iota((ATT_BLOCK, ATT_BLOCK), 1)
    same_chunk_causal = (r_i // HG_CHUNK == c_i // HG_CHUNK) & (c_i <= r_i)
    tri = jnp.where(same_chunk_causal, 1.0, 0.0).astype(BF16)
    lf_hi = log_f.astype(BF16)
    lf_lo = (log_f - lf_hi.astype(F32)).astype(BF16)
    b = jnp.concatenate(
        [jnp.dot(tri, lf_hi[s:s + ATT_BLOCK], preferred_element_type=F32)
         + jnp.dot(tri, lf_lo[s:s + ATT_BLOCK], preferred_element_type=F32)
         for s in range(0, rows, ATT_BLOCK)], axis=0)

    b3 = b.reshape(n_chunks, HG_CHUNK, width)
    b_ref = b3[:, HG_CHUNK // 2:HG_CHUNK // 2 + 1, :]
    b_last = b3[:, HG_CHUNK - 1:HG_CHUNK, :]
    q3 = q.reshape(n_chunks, HG_CHUNK, width)
    k3 = k.reshape(n_chunks, HG_CHUNK, width)
    qt = (q3 * jnp.exp(b3 - b_ref)).reshape(rows, width)
    kt = (k3 * jnp.exp(b_ref - b3)).reshape(rows, width)
    qd = (q3 * jnp.exp(b3)).reshape(rows, width)
    kd = (k3 * jnp.exp(b_last - b3)).reshape(rows, width)
    dec = jnp.exp(b_last)

    lane_head = _iota((ATT_BLOCK, width), 1) // HEAD_DIM
    o_parts = []
    for s in range(0, rows, ATT_BLOCK):
        qh, kh, vh = qt[s:s + ATT_BLOCK], kt[s:s + ATT_BLOCK], iv[s:s + ATT_BLOCK]
        o_blk = jnp.zeros((ATT_BLOCK, width), F32)
        for hd in range(N_HEADS):
            att = _dot_nt(jnp.where(lane_head == hd, qh, 0.0), kh)
            att = jnp.where(same_chunk_causal, att, 0.0)
            o_blk = jnp.where(lane_head == hd, _dot(att, vh), o_blk)
        o_parts.append(o_blk)
    o_intra = jnp.concatenate(o_parts, axis=0)

    st_r = _iota((width, width), 0) // HEAD_DIM
    st_c = _iota((width, width), 1) // HEAD_DIM
    same_head = st_r == st_c
    state = st_ref[...]
    o_parts = []
    for i in range(n_chunks):
        sl = slice(i * HG_CHUNK, (i + 1) * HG_CHUNK)
        o_parts.append(_dot_nt(qd[sl], state))
        d_state = jnp.where(same_head, _dot_tn(iv[sl], kd[sl]), 0.0)
        state = state * dec[i] + d_state
    st_ref[...] = state
    o = o_intra + jnp.concatenate(o_parts, axis=0)

    ms = _dot_split(o * o, _head_mean_matrix(width))
    o = (o * lax.rsqrt(ms + EPS)) * out_w
    return o * jax.nn.sigmoid(g)


def _retention_block(q, k, v, g, gn_w, gn_b, st_ref):
    rows, width = q.shape
    cl = ATT_BLOCK
    lane_head_row = _iota((1, width), 1) // HEAD_DIM
    log_gamma_row = jnp.zeros((1, width), F32)
    for hd in range(N_HEADS):
        log_gamma_row = jnp.where(lane_head_row == hd, math.log1p(-2.0 ** (-5.0 - hd)), log_gamma_row)
    pos_col = _iota((cl, 1), 0).astype(F32)
    rel = _iota((cl, cl), 0) - _iota((cl, cl), 1)
    rel_f = jnp.maximum(rel, 0).astype(F32)
    lane_head = _iota((cl, width), 1) // HEAD_DIM
    st_r = _iota((width, width), 0) // HEAD_DIM
    st_c = _iota((width, width), 1) // HEAD_DIM
    same_head = st_r == st_c
    q_in_decay = jnp.exp(log_gamma_row * (pos_col + 1.0))
    k_out_decay = jnp.exp(log_gamma_row * (cl - 1.0 - pos_col))
    chunk_decay = jnp.exp(log_gamma_row * float(cl))
    decays = [jnp.where(rel >= 0, jnp.exp(math.log1p(-2.0 ** (-5.0 - hd)) * rel_f), 0.0)
              for hd in range(N_HEADS)]

    state = st_ref[...]
    o_parts = []
    for s in range(0, rows, cl):
        qs = q[s:s + cl] * HEAD_DIM ** -0.5
        kc, vc = k[s:s + cl], v[s:s + cl]
        o_blk = _dot_nt(qs * q_in_decay, state)
        for hd in range(N_HEADS):
            att = _dot_nt(jnp.where(lane_head == hd, qs, 0.0), kc) * decays[hd]
            o_blk = o_blk + jnp.where(lane_head == hd, _dot(att, vc), 0.0)
        o_parts.append(o_blk)
        d_state = jnp.where(same_head, _dot_tn(vc, kc * k_out_decay), 0.0)
        state = state * chunk_decay + d_state
    st_ref[...] = state
    o = jnp.concatenate(o_parts, axis=0)

    mean_m = _head_mean_matrix(width)
    mu = _dot_split(o, mean_m)
    d = o - mu
    var = _dot_split(d * d, mean_m)
    o = d * lax.rsqrt(var + EPS) * gn_w + gn_b
    return o * jax.nn.silu(g)


def _swa_block(q, k, v, qn_w, kn_w, sinks_ref, kprev_ref, vprev_ref, block_index):
    rows = q.shape[0]
    kv_w = ATT_KV_HEADS * HEAD_DIM
    qn = (q * lax.rsqrt(_dot_split(q * q, _head_mean_matrix(q.shape[1])) + EPS)) * qn_w
    kn = (k * lax.rsqrt(_dot_split(k * k, _head_mean_matrix(kv_w)) + EPS)) * kn_w
    qb = ATT_BLOCK
    dist = _iota((qb, 2 * qb), 0) + qb - _iota((qb, 2 * qb), 1)
    in_window = (dist >= 0) & (dist < WINDOW)
    has_key = _iota((qb, 2 * qb), 1) + block_index * (2 * qb) >= qb
    dist_f = dist.astype(F32)
    lane_kv = _iota((qb, kv_w), 1) // HEAD_DIM
    k_prev, v_prev = kprev_ref[...], vprev_ref[...]
    o_parts = []
    for j, s in enumerate(range(0, rows, qb)):
        k_cur, v_cur = kn[s:s + qb], v[s:s + qb]
        kw = jnp.concatenate([k_prev, k_cur], axis=0)
        vw = jnp.concatenate([v_prev, v_cur], axis=0)
        valid = in_window if j > 0 else in_window & has_key
        o_tiles = []
        for tile in range(2):
            q_tile = qn[s:s + qb, tile * kv_w:(tile + 1) * kv_w]
            o_tile = jnp.zeros((qb, kv_w), F32)
            for kv in range(ATT_KV_HEADS):
                head = kv * 2 + tile
                slope = 2.0 ** (-8.0 * (head + 1) / N_HEADS)
                sc = _dot_nt(jnp.where(lane_kv == kv, q_tile, 0.0), kw) * HEAD_DIM ** -0.5
                sc = jnp.where(valid, sc - slope * dist_f, MASK_VALUE)
                sink = sinks_ref[head]
                m = jnp.maximum(jnp.max(sc, axis=-1, keepdims=True), sink)
                e = jnp.exp(sc - m)
                denom = jnp.sum(e, axis=-1, keepdims=True) + jnp.exp(sink - m)
                p = e / denom
                o_tile = jnp.where(lane_kv == kv, _dot(p, vw), o_tile)
            o_tiles.append(o_tile)
        o_parts.append(jnp.concatenate(o_tiles, axis=1))
        k_prev, v_prev = k_cur, v_cur
    kprev_ref[...] = k_prev
    vprev_ref[...] = v_prev
    return jnp.concatenate(o_parts, axis=0)


def _rglru_block(xb, gb, conv_w, conv_b, wa, ba, wx, bx, lam, xpad_ref, h_ref):
    rows, width = xb.shape
    xpad_ref[SUBLANES:, :] = xb
    xc = conv_b
    for j in range(CONV_WIDTH):
        start = SUBLANES - (CONV_WIDTH - 1) + j
        xc = xc + xpad_ref[start:start + rows, :] * conv_w[j:j + 1, :]
    xpad_ref[0:SUBLANES, :] = xb[rows - SUBLANES:rows, :]

    r = jax.nn.sigmoid(_dot(xc, wa) + ba)
    i = jax.nn.sigmoid(_dot(xc, wx) + bx)
    log_a = -LRU_C * r * _softplus(-lam)
    a = jnp.exp(log_a)
    u = jnp.sqrt(-_expm1_nonpositive(2.0 * log_a)) * (i * xc)

    row = _iota((rows, width), 0)
    shift = 1
    while shift < rows:
        a_prev = pltpu.roll(a, shift, axis=0)
        u_prev = pltpu.roll(u, shift, axis=0)
        take = row >= shift
        u = jnp.where(take, a * u_prev + u, u)
        a = jnp.where(take, a * a_prev, a)
        shift *= 2
    h = u + a * h_ref[...]
    h_ref[...] = h[rows - 1:rows, :]
    return h * jax.nn.gelu(gb)


def _mixer_body(layer, x_ref, nw_ref, win_ref, gbias_ref, lbl_ref, hgw_ref, gnw_ref, gnb_ref,
                qnw_ref, knw_ref, sinks_ref, cw_ref, cb_ref, wa_ref, ba_ref, wx_ref, bx_ref, lam_ref,
                wbr_ref, wout_ref, o_ref,
                hg_state, ret_state, kprev, vprev, xpad, lru_h):
    t = pl.program_id(1)

    @pl.when(t == 0)
    def _():
        hg_state[...] = jnp.zeros_like(hg_state)
        ret_state[...] = jnp.zeros_like(ret_state)
        kprev[...] = jnp.zeros_like(kprev)
        vprev[...] = jnp.zeros_like(vprev)
        xpad[...] = jnp.zeros_like(xpad)
        lru_h[...] = jnp.zeros_like(lru_h)

    x = x_ref[...]
    h = _rmsnorm_rows(x, nw_ref[...]).astype(BF16)
    w = BRANCH_WIDTH

    def proj(start, size):
        return jnp.dot(h, win_ref[:, start:start + size], preferred_element_type=F32)

    logits = lbl_ref[...]
    e = jnp.exp(logits - jnp.max(logits, axis=0, keepdims=True))
    p = e / jnp.sum(e, axis=0, keepdims=True)
    cum = p[0:1, :]
    for j in range(1, layer + 1):
        cum = cum + p[j:j + 1, :]
    lb = cum - p[0:1, :]

    y_a = _hgrn2_block(proj(0, w), proj(w, w), proj(2 * w, w), proj(3 * w, w), lb, hgw_ref[...], hg_state)
    y_b = _retention_block(proj(4 * w, w), proj(5 * w, w), proj(6 * w, w), proj(7 * w, w),
                           gnw_ref[...], gnb_ref[...], ret_state)
    kv_w = ATT_KV_HEADS * HEAD_DIM
    y_c = _swa_block(proj(8 * w, w), proj(9 * w, kv_w), proj(9 * w + kv_w, kv_w),
                     qnw_ref[...], knw_ref[...], sinks_ref, kprev, vprev, t)
    off = 9 * w + 2 * kv_w
    y_d = _rglru_block(proj(off, w), proj(off + w, w), cw_ref[...], cb_ref[...], wa_ref[...], ba_ref[...],
                       wx_ref[...], bx_ref[...], lam_ref[...], xpad, lru_h)
    off = off + 2 * w

    d = x.shape[1]
    merged = jnp.zeros(x.shape, F32)
    for n, y in enumerate((y_a, y_b, y_c, y_d)):
        gate = jax.nn.sigmoid(proj(off + n * d, d) + gbias_ref[n:n + 1, :])
        merged = merged + gate * _dot(y, wbr_ref[n])
    o_ref[...] = x + _dot(merged, wout_ref[...])


def _mixer(x2, n_seq, layer, norm_w, w_in, gate_bias, lb_logits, hg_w, gn_w, gn_b, qn_w, kn_w, sinks,
           conv_w, conv_b, wa, ba, wx, bx, lam, w_branch, w_out):
    m, d = x2.shape
    seq = m // n_seq
    assert seq % MIX_ROWS == 0 and MIX_ROWS % ATT_BLOCK == 0
    steps = seq // MIX_ROWS
    w = BRANCH_WIDTH
    kv_w = ATT_KV_HEADS * HEAD_DIM
    row = lambda v: v.reshape(1, -1)
    operands = [
        x2, row(norm_w), w_in, gate_bias, lb_logits, row(hg_w), row(gn_w), row(gn_b),
        row(jnp.tile(qn_w, N_HEADS)), row(jnp.tile(kn_w, ATT_KV_HEADS)), sinks,
        conv_w, row(conv_b), wa, row(ba), wx, row(bx), row(lam), w_branch, w_out,
    ]
    in_specs = [pl.BlockSpec((MIX_ROWS, d), lambda b, t: (b * steps + t, 0))]
    for op in operands[1:]:
        if op is sinks:
            in_specs.append(pl.BlockSpec(memory_space=pltpu.SMEM))
        else:
            in_specs.append(_const_spec(op.shape))
    return pl.pallas_call(
        functools.partial(_mixer_body, layer),
        grid=(n_seq, steps),
        in_specs=in_specs,
        out_specs=pl.BlockSpec((MIX_ROWS, d), lambda b, t: (b * steps + t, 0)),
        out_shape=jax.ShapeDtypeStruct((m, d), F32),
        scratch_shapes=[
            pltpu.VMEM((w, w), F32),
            pltpu.VMEM((w, w), F32),
            pltpu.VMEM((ATT_BLOCK, kv_w), F32),
            pltpu.VMEM((ATT_BLOCK, kv_w), F32),
            pltpu.VMEM((SUBLANES + MIX_ROWS, w), F32),
            pltpu.VMEM((1, w), F32),
        ],
        compiler_params=pltpu.CompilerParams(
            dimension_semantics=("arbitrary", "arbitrary"), vmem_limit_bytes=VMEM_LIMIT_BYTES),
        name="token_mixer",
    )(*operands)


def _block_diag(blocks):
    n, c, e = blocks.shape
    eye = jnp.eye(n, dtype=blocks.dtype)
    return (eye[:, None, :, None] * blocks[:, :, None, :]).reshape(n * c, n * e)


def _swa_head_order(t, axis):
    parts = jnp.split(t, N_HEADS, axis=axis)
    return jnp.concatenate([parts[0], parts[2], parts[1], parts[3]], axis=axis)


def kernel(x, ffn1_norm, ffn1_wg, ffn1_wu, ffn1_wd, mix_norm, w_in, gate_bias, hgrn_lb_logits, hgrn_out_norm, ret_gn_w, ret_gn_b, attn_q_norm, attn_k_norm, attn_sinks, lru_conv_w, lru_conv_b, lru_wa, lru_ba, lru_wx, lru_bx, lru_lambda, w_branch, w_out, ffn2_norm, ffn2_wg, ffn2_wu, ffn2_wd):
    n_seq, seq, d = x.shape
    depth = w_in.shape[0]
    w = BRANCH_WIDTH
    x2 = x.reshape(n_seq * seq, d)
    aq = 8 * w
    for l in range(depth):
        x2 = _ffn(x2, ffn1_norm[l], ffn1_wg[l].astype(BF16), ffn1_wu[l].astype(BF16), ffn1_wd[l].astype(BF16))
        win = w_in[l]
        win = jnp.concatenate([win[:, :aq], _swa_head_order(win[:, aq:aq + w], 1), win[:, aq + w:]], axis=1)
        wbr = w_branch[l]
        wbr = jnp.concatenate([wbr[:2], _swa_head_order(wbr[2], 0)[None], wbr[3:]], axis=0)
        x2 = _mixer(
            x2, n_seq, l, mix_norm[l], win.astype(BF16), gate_bias[l], hgrn_lb_logits, hgrn_out_norm[l],
            ret_gn_w[l], ret_gn_b[l], attn_q_norm[l], attn_k_norm[l], attn_sinks[l],
            lru_conv_w[l], lru_conv_b[l], _block_diag(lru_wa[l]).astype(BF16), lru_ba[l],
            _block_diag(lru_wx[l]).astype(BF16), lru_bx[l], lru_lambda[l], wbr.astype(BF16), w_out[l].astype(BF16))
        x2 = _ffn(x2, ffn2_norm[l], ffn2_wg[l].astype(BF16), ffn2_wu[l].astype(BF16), ffn2_wd[l].astype(BF16))
    return x2.reshape(n_seq, seq, d)
```

```python
import functools
import math

import jax
import jax.numpy as jnp
from jax import lax
from jax.experimental import pallas as pl
from jax.experimental.pallas import tpu as pltpu

HEAD_DIM = 64
BRANCH_WIDTH = 256
N_BRANCH = 4
N_HEADS = 4
HG_CHUNK = 16
LB_FLOOR = 1e-30
ATT_KV_HEADS = 2
ATT_BLOCK = 128
WINDOW = 128
MASK_VALUE = -1e30
CONV_WIDTH = 4
LRU_C = 8.0
EPS = 1e-6

V7X_VMEM_BYTES = 64 * 1024 * 1024
VMEM_LIMIT_BYTES = V7X_VMEM_BYTES - 8 * 1024 * 1024
SUBLANES = 8

FFN_ROWS = 512
FFN_COLS = 256
MIX_ROWS = 256
GATE_COLS = 512

BF16 = jnp.bfloat16
F32 = jnp.float32


def _dot(a, b):
    return jnp.dot(a.astype(BF16), b.astype(BF16), preferred_element_type=F32)


def _dot_nt(a, b):
    return lax.dot_general(a.astype(BF16), b.astype(BF16), (((1,), (1,)), ((), ())),
                           preferred_element_type=F32)


def _dot_tn(a, b):
    return lax.dot_general(a.astype(BF16), b.astype(BF16), (((0,), (0,)), ((), ())),
                           preferred_element_type=F32)


def _dot_split(a, m):
    hi = a.astype(BF16)
    lo = (a - hi.astype(F32)).astype(BF16)
    return (jnp.dot(hi, m, preferred_element_type=F32) + jnp.dot(lo, m, preferred_element_type=F32))


def _rmsnorm_rows(x, w_row):
    return (x * lax.rsqrt(jnp.mean(x * x, axis=-1, keepdims=True) + EPS)) * w_row


def _softplus(x):
    return jnp.maximum(x, 0.0) + jnp.log1p(jnp.exp(-jnp.abs(x)))


def _sigmoid(x):
    return 0.5 * jnp.tanh(0.5 * x) + 0.5


def _iota(shape, axis):
    return lax.broadcasted_iota(jnp.int32, shape, axis)


def _ffn_body(x_ref, nw_ref, wg_ref, wu_ref, wd_ref, o_ref):
    x = x_ref[...]
    h = _rmsnorm_rows(x, nw_ref[...]).astype(BF16)
    d_ff = wg_ref.shape[1]
    acc = jnp.zeros(x.shape, F32)
    for c in range(d_ff // FFN_COLS):
        cols = slice(c * FFN_COLS, (c + 1) * FFN_COLS)
        g = jnp.dot(h, wg_ref[:, cols], preferred_element_type=F32)
        u = jnp.dot(h, wu_ref[:, cols], preferred_element_type=F32)
        a = (jax.nn.silu(g) * u).astype(BF16)
        acc = acc + jnp.dot(a, wd_ref[cols, :], preferred_element_type=F32)
    o_ref[...] = x + 0.5 * acc


def _const_spec(shape):
    zeros = (0,) * len(shape)
    return pl.BlockSpec(shape, lambda *_: zeros, pipeline_mode=pl.Buffered(1))


def _ffn(x2, norm_w, wg, wu, wd):
    m, d = x2.shape
    d_ff = wg.shape[1]
    assert m % FFN_ROWS == 0 and d_ff % FFN_COLS == 0
    return pl.pallas_call(
        _ffn_body,
        grid=(m // FFN_ROWS,),
        in_specs=[
            pl.BlockSpec((FFN_ROWS, d), lambda i: (i, 0)),
            _const_spec((1, d)),
            _const_spec((d, d_ff)),
            _const_spec((d, d_ff)),
            _const_spec((d_ff, d)),
        ],
        out_specs=pl.BlockSpec((FFN_ROWS, d), lambda i: (i, 0)),
        out_shape=jax.ShapeDtypeStruct((m, d), F32),
        compiler_params=pltpu.CompilerParams(
            dimension_semantics=("arbitrary",), vmem_limit_bytes=VMEM_LIMIT_BYTES),
        name="channel_mixer",
    )(x2, norm_w.reshape(1, d), wg, wu, wd)


class _WorkQueue:
    def __init__(self):
        self._pending = []
        self._done = {}

    def add(self, key, thunk):
        self._pending.append((key, thunk))

    def step(self, count=1):
        for _ in range(min(count, len(self._pending))):
            key, thunk = self._pending.pop(0)
            self._done[key] = thunk()

    def get(self, key):
        while key not in self._done:
            self.step()
        return self._done[key]

    def flush(self):
        self.step(len(self._pending))


def _head_mean_matrix(width):
    r = _iota((width, width), 0) // HEAD_DIM
    c = _iota((width, width), 1) // HEAD_DIM
    return jnp.where(r == c, 1.0 / HEAD_DIM, 0.0).astype(BF16)


def _hgrn2_block(q, z, iv, g, lb, out_w, st_ref, work):
    rows, width = q.shape
    n_chunks = rows // HG_CHUNK
    log_lb = jnp.log(jnp.maximum(lb, LB_FLOOR))
    log_sig = -_softplus(-z)
    work.step()
    c = jnp.log1p(-lb) + log_sig
    log_f = jnp.maximum(log_lb, c) + jnp.log1p(jnp.exp(-jnp.abs(log_lb - c)))
    work.step()
    k = (1.0 - lb) * _sigmoid(-z)

    r_i = _iota((ATT_BLOCK, ATT_BLOCK), 0)
    c_i = _iota((ATT_BLOCK, ATT_BLOCK), 1)
    same_chunk_causal = (r_i // HG_CHUNK == c_i // HG_CHUNK) & (c_i <= r_i)
    tri = jnp.where(same_chunk_causal, 1.0, 0.0).astype(BF16)
    lf_hi = log_f.astype(BF16)
    lf_lo = (log_f - lf_hi.astype(F32)).astype(BF16)
    b = jnp.concatenate(
        [jnp.dot(tri, lf_hi[s:s + ATT_BLOCK], preferred_element_type=F32)
         + jnp.dot(tri, lf_lo[s:s + ATT_BLOCK], preferred_element_type=F32)
         for s in range(0, rows, ATT_BLOCK)], axis=0)
    work.step()

    b3 = b.reshape(n_chunks, HG_CHUNK, width)
    b_ref = b3[:, HG_CHUNK // 2:HG_CHUNK // 2 + 1, :]
    b_last = b3[:, HG_CHUNK - 1:HG_CHUNK, :]
    q3 = q.reshape(n_chunks, HG_CHUNK, width)
    k3 = k.reshape(n_chunks, HG_CHUNK, width)
    qt = (q3 * jnp.exp(b3 - b_ref)).reshape(rows, width)
    work.step()
    kt = (k3 * jnp.exp(b_ref - b3)).reshape(rows, width)
    work.step()
    qd = (q3 * jnp.exp(b3)).reshape(rows, width)
    work.step()
    kd = (k3 * jnp.exp(b_last - b3)).reshape(rows, width)
    work.step()
    dec = jnp.exp(b_last)

    lane_head = _iota((ATT_BLOCK, width), 1) // HEAD_DIM
    mask_stack = jnp.concatenate([same_chunk_causal] * N_HEADS, axis=0)
    o_parts = []
    for s in range(0, rows, ATT_BLOCK):
        qh, kh, vh = qt[s:s + ATT_BLOCK], kt[s:s + ATT_BLOCK], iv[s:s + ATT_BLOCK]
        q_stack = jnp.concatenate([jnp.where(lane_head == hd, qh, 0.0) for hd in range(N_HEADS)], axis=0)
        att = jnp.where(mask_stack, _dot_nt(q_stack, kh), 0.0)
        o_full = _dot(att, vh)
        o_blk = o_full[0:ATT_BLOCK]
        for hd in range(1, N_HEADS):
            o_blk = jnp.where(lane_head == hd, o_full[hd * ATT_BLOCK:(hd + 1) * ATT_BLOCK], o_blk)
        o_parts.append(o_blk)
        work.step()
    o_intra = jnp.concatenate(o_parts, axis=0)

    chunk_lane_head = _iota((HG_CHUNK, width), 1) // HEAD_DIM
    v_heads = [iv[:, hd * HEAD_DIM:(hd + 1) * HEAD_DIM] for hd in range(N_HEADS)]
    increments = []
    for i in range(n_chunks):
        sl = slice(i * HG_CHUNK, (i + 1) * HG_CHUNK)
        k_stack = jnp.concatenate([jnp.where(chunk_lane_head == hd, kd[sl], 0.0) for hd in range(N_HEADS)], axis=0)
        v_stack = jnp.concatenate([vh[sl] for vh in v_heads], axis=0)
        increments.append(_dot_tn(v_stack, k_stack))
        if i % 8 == 7:
            work.step()
    state = st_ref[...]
    states = []
    for i in range(n_chunks):
        states.append(state)
        state = state * dec[i] + increments[i]
    st_ref[...] = state
    p_parts = []
    for i in range(n_chunks):
        sl = slice(i * HG_CHUNK, (i + 1) * HG_CHUNK)
        q_stack = jnp.concatenate([jnp.where(chunk_lane_head == hd, qd[sl], 0.0) for hd in range(N_HEADS)], axis=0)
        p_parts.append(_dot_nt(q_stack, states[i]))
        if i % 8 == 7:
            work.step()
    o_inter = jnp.concatenate(
        [jnp.concatenate([p[hd * HG_CHUNK:(hd + 1) * HG_CHUNK] for p in p_parts], axis=0) for hd in range(N_HEADS)],
        axis=1)
    o = o_intra + o_inter

    ms = _dot_split(o * o, _head_mean_matrix(width))
    o = (o * lax.rsqrt(ms + EPS)) * out_w
    return o * _sigmoid(g)


def _retention_block(q, k, v, g, gn_w, gn_b, st_ref, work):
    rows, width = q.shape
    cl = ATT_BLOCK
    lane_head_row = _iota((1, width), 1) // HEAD_DIM
    log_gamma_row = jnp.zeros((1, width), F32)
    for hd in range(N_HEADS):
        log_gamma_row = jnp.where(lane_head_row == hd, math.log1p(-2.0 ** (-5.0 - hd)), log_gamma_row)
    pos_col = _iota((cl, width), 0).astype(F32)
    rel = _iota((cl, cl), 0) - _iota((cl, cl), 1)
    rel_f = jnp.maximum(rel, 0).astype(F32)
    lane_head = _iota((cl, width), 1) // HEAD_DIM
    st_r = _iota((width, width), 0) // HEAD_DIM
    st_c = _iota((width, width), 1) // HEAD_DIM
    same_head = st_r == st_c
    q_in_decay = jnp.exp(log_gamma_row * (pos_col + 1.0))
    k_out_decay = jnp.exp(log_gamma_row * (cl - 1.0 - pos_col))
    chunk_decay = jnp.exp(log_gamma_row * float(cl))
    decay_stack = jnp.concatenate(
        [jnp.where(rel >= 0, jnp.exp(math.log1p(-2.0 ** (-5.0 - hd)) * rel_f), 0.0) for hd in range(N_HEADS)],
        axis=0)

    state = st_ref[...]
    o_parts = []
    for s in range(0, rows, cl):
        qs = q[s:s + cl] * HEAD_DIM ** -0.5
        kc, vc = k[s:s + cl], v[s:s + cl]
        o_blk = _dot_nt(qs * q_in_decay, state)
        q_stack = jnp.concatenate([jnp.where(lane_head == hd, qs, 0.0) for hd in range(N_HEADS)], axis=0)
        o_full = _dot(_dot_nt(q_stack, kc) * decay_stack, vc)
        for hd in range(N_HEADS):
            o_blk = o_blk + jnp.where(lane_head == hd, o_full[hd * cl:(hd + 1) * cl], 0.0)
        o_parts.append(o_blk)
        d_state = jnp.where(same_head, _dot_tn(vc, kc * k_out_decay), 0.0)
        state = state * chunk_decay + d_state
        work.step()
    st_ref[...] = state
    o = jnp.concatenate(o_parts, axis=0)

    mean_m = _head_mean_matrix(width)
    mu = _dot_split(o, mean_m)
    d = o - mu
    var = _dot_split(d * d, mean_m)
    work.step()
    o = d * lax.rsqrt(var + EPS) * gn_w + gn_b
    return o * (g * _sigmoid(g))


def _swa_block(q, k, v, qn_w, kn_w, sinks_ref, kprev_ref, vprev_ref, block_index, work):
    rows = q.shape[0]
    kv_w = ATT_KV_HEADS * HEAD_DIM
    qn = (q * lax.rsqrt(_dot_split(q * q, _head_mean_matrix(q.shape[1])) + EPS)) * qn_w
    kn = (k * lax.rsqrt(_dot_split(k * k, _head_mean_matrix(kv_w)) + EPS)) * kn_w
    qb = ATT_BLOCK
    dist = _iota((qb, 2 * qb), 0) + qb - _iota((qb, 2 * qb), 1)
    in_window = (dist >= 0) & (dist < WINDOW)
    has_key = _iota((qb, 2 * qb), 1) + block_index * (2 * qb) >= qb
    dist_f = dist.astype(F32)
    lane_kv = _iota((qb, kv_w), 1) // HEAD_DIM
    n_blocks = rows // qb

    k_prev, v_prev = kprev_ref[...], vprev_ref[...]
    scores, values, sink_of = [], [], []
    for j in range(n_blocks):
        s = j * qb
        k_cur, v_cur = kn[s:s + qb], v[s:s + qb]
        kw = jnp.concatenate([k_prev, k_cur], axis=0)
        values.append(jnp.concatenate([v_prev, v_cur], axis=0))
        valid = in_window if j > 0 else in_window & has_key
        for kv in range(ATT_KV_HEADS):
            q_stack = jnp.concatenate(
                [jnp.where(lane_kv == kv, qn[s:s + qb, tile * kv_w:(tile + 1) * kv_w], 0.0) for tile in range(2)],
                axis=0)
            sc = _dot_nt(q_stack, kw) * HEAD_DIM ** -0.5
            for tile in range(2):
                head = kv * 2 + tile
                slope = 2.0 ** (-8.0 * (head + 1) / N_HEADS)
                scores.append(jnp.where(valid, sc[tile * qb:(tile + 1) * qb] - slope * dist_f, MASK_VALUE))
                sink_of.append(sinks_ref[head])
        k_prev, v_prev = k_cur, v_cur
        work.step()
    kprev_ref[...] = k_prev
    vprev_ref[...] = v_prev

    maxes = [jnp.maximum(jnp.max(sc, axis=-1, keepdims=True), sink) for sc, sink in zip(scores, sink_of)]
    exps = [jnp.exp(sc - m) for sc, m in zip(scores, maxes)]
    work.step()
    probs = [e / (jnp.sum(e, axis=-1, keepdims=True) + jnp.exp(sink - m))
             for e, m, sink in zip(exps, maxes, sink_of)]

    o_parts = []
    for j in range(n_blocks):
        o_kv = [_dot(jnp.concatenate(probs[(j * ATT_KV_HEADS + kv) * 2:(j * ATT_KV_HEADS + kv + 1) * 2], axis=0),
                     values[j]) for kv in range(ATT_KV_HEADS)]
        o_tiles = [jnp.where(lane_kv == 0, o_kv[0][tile * qb:(tile + 1) * qb], o_kv[1][tile * qb:(tile + 1) * qb])
                   for tile in range(2)]
        o_parts.append(jnp.concatenate(o_tiles, axis=1))
        work.step()
    return jnp.concatenate(o_parts, axis=0)


def _rglru_block(xb, gb, conv_w, conv_b, wa, ba, wx, bx, lam, xpad_ref, h_ref, work):
    rows, width = xb.shape
    padded = jnp.concatenate([xpad_ref[...], xb], axis=0)
    xc = conv_b
    for j in range(CONV_WIDTH):
        back = CONV_WIDTH - 1 - j
        shifted = pltpu.roll(padded, back, axis=0) if back else padded
        xc = xc + shifted[SUBLANES:] * conv_w[j:j + 1, :]
    xpad_ref[...] = xb[rows - SUBLANES:rows, :]
    work.step()

    r = _sigmoid(_dot(xc, wa) + ba)
    i = _sigmoid(_dot(xc, wx) + bx)
    work.step()
    log_a = -LRU_C * r * _softplus(-lam)
    a = jnp.exp(log_a)
    u = jnp.sqrt(-jnp.tanh(log_a) * (a * a + 1.0)) * (i * xc)
    work.step()

    row = _iota((rows, width), 0)
    shift = 1
    while shift < rows:
        a_prev = pltpu.roll(a, shift, axis=0)
        u_prev = pltpu.roll(u, shift, axis=0)
        take = row >= shift
        u = jnp.where(take, a * u_prev + u, u)
        a = jnp.where(take, a * a_prev, a)
        shift *= 2
        if shift in (2, 8, 32, 128):
            work.step()
    h = u + a * h_ref[...]
    h_ref[...] = h[rows - 1:rows, :]
    return h * jax.nn.gelu(gb)


def _mixer_body(layer, x_ref, nw_ref, win_ref, gbias_ref, lbl_ref, hgw_ref, gnw_ref, gnb_ref,
                qnw_ref, knw_ref, sinks_ref, cw_ref, cb_ref, wa_ref, ba_ref, wx_ref, bx_ref, lam_ref,
                wbr_ref, wout_ref, o_ref,
                hg_state, ret_state, kprev, vprev, xpad, lru_h, gates, ys, h_s):
    t = pl.program_id(1)

    @pl.when(t == 0)
    def _():
        hg_state[...] = jnp.zeros_like(hg_state)
        ret_state[...] = jnp.zeros_like(ret_state)
        kprev[...] = jnp.zeros_like(kprev)
        vprev[...] = jnp.zeros_like(vprev)
        xpad[...] = jnp.zeros_like(xpad)
        lru_h[...] = jnp.zeros_like(lru_h)

    d = x_ref.shape[1]
    h_s[...] = _rmsnorm_rows(x_ref[...], nw_ref[...]).astype(BF16)
    w = BRANCH_WIDTH
    kv_w = ATT_KV_HEADS * HEAD_DIM

    def proj(start, size):
        return jnp.dot(h_s[...], win_ref[:, start:start + size], preferred_element_type=F32)

    logits = lbl_ref[...]
    e = jnp.exp(logits - jnp.max(logits, axis=0, keepdims=True))
    p = e / jnp.sum(e, axis=0, keepdims=True)
    cum = p[0:1, :]
    for j in range(1, layer + 1):
        cum = cum + p[j:j + 1, :]
    lb = cum - p[0:1, :]

    names = ("hq", "hf", "hi", "hg", "rq", "rk", "rv", "rg", "aq", "ak", "av", "lx", "lg")
    sizes = (w, w, w, w, w, w, w, w, w, kv_w, kv_w, w, w)
    offsets = {}
    off = 0
    for name, size in zip(names, sizes):
        offsets[name] = (off, size)
        off += size
    gate_off = off

    def proj_group(*group):
        start = offsets[group[0]][0]
        stop = sum(offsets[group[-1]])
        both = proj(start, stop - start)
        return [both[:, offsets[name][0] - start:sum(offsets[name]) - start] for name in group]

    def gate_chunk(c):
        cols = slice(c * GATE_COLS, (c + 1) * GATE_COLS)
        gates[:, cols] = _sigmoid(proj(gate_off + c * GATE_COLS, GATE_COLS) + gbias_ref[:, cols])

    gate_chunks = iter(range(N_BRANCH * d // GATE_COLS))

    def phase(index, n_gate_chunks, mixer):
        work = _WorkQueue()
        for c in [next(gate_chunks) for _ in range(n_gate_chunks)]:
            work.add(("gate", c), functools.partial(gate_chunk, c))

        @pl.when(t >= -index)
        def _():
            mixer(work)
            work.flush()

    def lru_phase(work):
        ys[3] = _rglru_block(*proj_group("lx", "lg"), cw_ref[...], cb_ref[...], wa_ref[...],
                             ba_ref[...], wx_ref[...], bx_ref[...], lam_ref[...], xpad, lru_h, work).astype(BF16)

    def swa_phase(work):
        ys[2] = _swa_block(*proj_group("aq", "ak", "av"),
                           qnw_ref[...], knw_ref[...], sinks_ref, kprev, vprev, t, work).astype(BF16)

    def hgrn_phase(work):
        ys[0] = _hgrn2_block(*proj_group("hq", "hf", "hi", "hg"), lb, hgw_ref[...], hg_state, work).astype(BF16)

    def ret_phase(work):
        ys[1] = _retention_block(*proj_group("rq", "rk", "rv", "rg"), gnw_ref[...], gnb_ref[...], ret_state, work).astype(BF16)

    def merge_phase(work):
        work.flush()
        merged = jnp.zeros(x_ref.shape, F32)
        for n in range(N_BRANCH):
            merged = merged + gates[:, n * d:(n + 1) * d] * jnp.dot(ys[n], wbr_ref[n], preferred_element_type=F32)
        o_ref[...] = x_ref[...] + _dot(merged, wout_ref[...])

    phase(0, 3, lru_phase)
    phase(1, 2, swa_phase)
    phase(2, 1, hgrn_phase)
    phase(3, 1, ret_phase)
    phase(4, 1, merge_phase)


def _mixer(x2, n_seq, layer, norm_w, w_in, gate_bias, lb_logits, hg_w, gn_w, gn_b, qn_w, kn_w, sinks,
           conv_w, conv_b, wa, ba, wx, bx, lam, w_branch, w_out):
    m, d = x2.shape
    seq = m // n_seq
    assert seq % MIX_ROWS == 0 and MIX_ROWS % ATT_BLOCK == 0
    steps = seq // MIX_ROWS
    w = BRANCH_WIDTH
    kv_w = ATT_KV_HEADS * HEAD_DIM
    row = lambda v: v.reshape(1, -1)
    operands = [
        x2, row(norm_w), w_in, row(gate_bias), lb_logits, row(hg_w), row(gn_w), row(gn_b),
        row(jnp.tile(qn_w, N_HEADS)), row(jnp.tile(kn_w, ATT_KV_HEADS)), sinks,
        conv_w, row(conv_b), wa, row(ba), wx, row(bx), row(lam), w_branch, w_out,
    ]
    in_specs = [pl.BlockSpec((MIX_ROWS, d), lambda b, t: (b * steps + t, 0))]
    for op in operands[1:]:
        if op is sinks:
            in_specs.append(pl.BlockSpec(memory_space=pltpu.SMEM))
        else:
            in_specs.append(_const_spec(op.shape))
    return pl.pallas_call(
        functools.partial(_mixer_body, layer),
        grid=(n_seq, steps),
        in_specs=in_specs,
        out_specs=pl.BlockSpec((MIX_ROWS, d), lambda b, t: (b * steps + t, 0)),
        out_shape=jax.ShapeDtypeStruct((m, d), F32),
        scratch_shapes=[
            pltpu.VMEM((HEAD_DIM, w), F32),
            pltpu.VMEM((w, w), F32),
            pltpu.VMEM((ATT_BLOCK, kv_w), F32),
            pltpu.VMEM((ATT_BLOCK, kv_w), F32),
            pltpu.VMEM((SUBLANES, w), F32),
            pltpu.VMEM((1, w), F32),
            pltpu.VMEM((MIX_ROWS, N_BRANCH * d), F32),
            pltpu.VMEM((N_BRANCH, MIX_ROWS, w), BF16),
            pltpu.VMEM((MIX_ROWS, d), BF16),
        ],
        compiler_params=pltpu.CompilerParams(
            dimension_semantics=("arbitrary", "arbitrary"), vmem_limit_bytes=VMEM_LIMIT_BYTES),
        name="token_mixer",
    )(*operands)


def _block_diag(blocks):
    n, c, e = blocks.shape
    eye = jnp.eye(n, dtype=blocks.dtype)
    return (eye[:, None, :, None] * blocks[:, :, None, :]).reshape(n * c, n * e)


def _swa_head_order(t, axis):
    parts = jnp.split(t, N_HEADS, axis=axis)
    return jnp.concatenate([parts[0], parts[2], parts[1], parts[3]], axis=axis)


def kernel(x, ffn1_norm, ffn1_wg, ffn1_wu, ffn1_wd, mix_norm, w_in, gate_bias, hgrn_lb_logits, hgrn_out_norm, ret_gn_w, ret_gn_b, attn_q_norm, attn_k_norm, attn_sinks, lru_conv_w, lru_conv_b, lru_wa, lru_ba, lru_wx, lru_bx, lru_lambda, w_branch, w_out, ffn2_norm, ffn2_wg, ffn2_wu, ffn2_wd):
    n_seq, seq, d = x.shape
    depth = w_in.shape[0]
    w = BRANCH_WIDTH
    x2 = x.reshape(n_seq * seq, d)
    aq = 8 * w
    for l in range(depth):
        x2 = _ffn(x2, ffn1_norm[l], ffn1_wg[l].astype(BF16), ffn1_wu[l].astype(BF16), ffn1_wd[l].astype(BF16))
        win = w_in[l]
        win = jnp.concatenate([win[:, :aq], _swa_head_order(win[:, aq:aq + w], 1), win[:, aq + w:]], axis=1)
        wbr = w_branch[l]
        wbr = jnp.concatenate([wbr[:2], _swa_head_order(wbr[2], 0)[None], wbr[3:]], axis=0)
        x2 = _mixer(
            x2, n_seq, l, mix_norm[l], win.astype(BF16), gate_bias[l], hgrn_lb_logits, hgrn_out_norm[l],
            ret_gn_w[l], ret_gn_b[l], attn_q_norm[l], attn_k_norm[l], attn_sinks[l],
            lru_conv_w[l], lru_conv_b[l], _block_diag(lru_wa[l]).astype(BF16), lru_ba[l],
            _block_diag(lru_wx[l]).astype(BF16), lru_bx[l], lru_lambda[l], wbr.astype(BF16), w_out[l].astype(BF16))
        x2 = _ffn(x2, ffn2_norm[l], ffn2_wg[l].astype(BF16), ffn2_wu[l].astype(BF16), ffn2_wd[l].astype(BF16))
    return x2.reshape(n_seq, seq, d)
```

```python
import functools
import math

import jax
import jax.numpy as jnp
from jax import lax
from jax.experimental import pallas as pl
from jax.experimental.pallas import tpu as pltpu

HEAD_DIM = 64
BRANCH_WIDTH = 256
N_BRANCH = 4
N_HEADS = 4
HG_CHUNK = 16
LB_FLOOR = 1e-30
ATT_KV_HEADS = 2
ATT_BLOCK = 128
WINDOW = 128
MASK_VALUE = -1e30
CONV_WIDTH = 4
LRU_C = 8.0
EPS = 1e-6

V7X_VMEM_BYTES = 64 * 1024 * 1024
VMEM_LIMIT_BYTES = V7X_VMEM_BYTES - 8 * 1024 * 1024
SUBLANES = 8

FFN_ROWS = 512
FFN_COLS = 256
MIX_ROWS = 512
GATE_COLS = 512
PROJ_COLS = 512

BF16 = jnp.bfloat16
F32 = jnp.float32


def _dot(a, b):
    return jnp.dot(a.astype(BF16), b.astype(BF16), preferred_element_type=F32)


def _dot_nt(a, b):
    return lax.dot_general(a.astype(BF16), b.astype(BF16), (((1,), (1,)), ((), ())),
                           preferred_element_type=F32)


def _dot_tn(a, b):
    return lax.dot_general(a.astype(BF16), b.astype(BF16), (((0,), (0,)), ((), ())),
                           preferred_element_type=F32)


def _dot_split(a, m):
    hi = a.astype(BF16)
    lo = (a - hi.astype(F32)).astype(BF16)
    return (jnp.dot(hi, m, preferred_element_type=F32) + jnp.dot(lo, m, preferred_element_type=F32))


def _rmsnorm_rows(x, w_row):
    return (x * lax.rsqrt(jnp.mean(x * x, axis=-1, keepdims=True) + EPS)) * w_row


def _softplus(x):
    return jnp.maximum(x, 0.0) + jnp.log1p(jnp.exp(-jnp.abs(x)))


def _sigmoid(x):
    return 0.5 * jnp.tanh(0.5 * x) + 0.5


def _iota(shape, axis):
    return lax.broadcasted_iota(jnp.int32, shape, axis)


def _ffn_body(x_ref, nw_ref, wg_ref, wu_ref, wd_ref, o_ref):
    x = x_ref[...]
    h = _rmsnorm_rows(x, nw_ref[...]).astype(BF16)
    d_ff = wg_ref.shape[1]
    acc = jnp.zeros(x.shape, F32)
    for c in range(d_ff // FFN_COLS):
        cols = slice(c * FFN_COLS, (c + 1) * FFN_COLS)
        g = jnp.dot(h, wg_ref[:, cols], preferred_element_type=F32)
        u = jnp.dot(h, wu_ref[:, cols], preferred_element_type=F32)
        a = (jax.nn.silu(g) * u).astype(BF16)
        acc = acc + jnp.dot(a, wd_ref[cols, :], preferred_element_type=F32)
    o_ref[...] = x + 0.5 * acc


def _layer_spec(stacked, layer):
    index = (layer,) + (0,) * (stacked.ndim - 1)
    return pl.BlockSpec((None,) + stacked.shape[1:], lambda *_: index, pipeline_mode=pl.Buffered(1))


def _rows(stacked):
    return stacked.reshape(stacked.shape[0], 1, -1)


def _ffn(x2, layer, norm_w, wg, wu, wd):
    m, d = x2.shape
    d_ff = wg.shape[2]
    assert m % FFN_ROWS == 0 and d_ff % FFN_COLS == 0
    operands = (_rows(norm_w), wg, wu, wd)
    return pl.pallas_call(
        _ffn_body,
        grid=(m // FFN_ROWS,),
        in_specs=[pl.BlockSpec((FFN_ROWS, d), lambda i: (i, 0))] + [_layer_spec(op, layer) for op in operands],
        out_specs=pl.BlockSpec((FFN_ROWS, d), lambda i: (i, 0)),
        out_shape=jax.ShapeDtypeStruct((m, d), F32),
        compiler_params=pltpu.CompilerParams(
            dimension_semantics=("arbitrary",), vmem_limit_bytes=VMEM_LIMIT_BYTES),
        name="channel_mixer",
    )(x2, *operands)


class _WorkQueue:
    def __init__(self):
        self._pending = []

    def add(self, thunk):
        self._pending.append(thunk)

    def step(self):
        if self._pending:
            self._pending.pop(0)()

    def flush(self):
        while self._pending:
            self.step()


def _head_mean_matrix(width):
    r = _iota((width, width), 0) // HEAD_DIM
    c = _iota((width, width), 1) // HEAD_DIM
    return jnp.where(r == c, 1.0 / HEAD_DIM, 0.0).astype(BF16)


def _hgrn2_block(q, z, iv, g, lb, out_w, st_ref, work):
    rows, width = q.shape
    n_chunks = rows // HG_CHUNK
    work.step()
    log_lb = jnp.log(jnp.maximum(lb, LB_FLOOR))
    log_sig = -_softplus(-z)
    c = jnp.log1p(-lb) + log_sig
    log_f = jnp.maximum(log_lb, c) + jnp.log1p(jnp.exp(-jnp.abs(log_lb - c)))
    k = (1.0 - lb) * _sigmoid(-z)

    r_i = _iota((ATT_BLOCK, ATT_BLOCK), 0)
    c_i = _iota((ATT_BLOCK, ATT_BLOCK), 1)
    same_chunk_causal = (r_i // HG_CHUNK == c_i // HG_CHUNK) & (c_i <= r_i)
    tri = jnp.where(same_chunk_causal, 1.0, 0.0).astype(BF16)
    lf_hi = log_f.astype(BF16)
    lf_lo = (log_f - lf_hi.astype(F32)).astype(BF16)
    b = jnp.concatenate(
        [jnp.dot(tri, lf_hi[s:s + ATT_BLOCK], preferred_element_type=F32)
         + jnp.dot(tri, lf_lo[s:s + ATT_BLOCK], preferred_element_type=F32)
         for s in range(0, rows, ATT_BLOCK)], axis=0)
    work.step()

    b3 = b.reshape(n_chunks, HG_CHUNK, width)
    b_ref = b3[:, HG_CHUNK // 2:HG_CHUNK // 2 + 1, :]
    b_last = b3[:, HG_CHUNK - 1:HG_CHUNK, :]
    q3 = q.reshape(n_chunks, HG_CHUNK, width)
    k3 = k.reshape(n_chunks, HG_CHUNK, width)
    qt = (q3 * jnp.exp(b3 - b_ref)).reshape(rows, width)
    kt = (k3 * jnp.exp(b_ref - b3)).reshape(rows, width)
    qd = (q3 * jnp.exp(b3)).reshape(rows, width)
    kd = (k3 * jnp.exp(b_last - b3)).reshape(rows, width)
    dec = jnp.exp(b_last)

    chunk_lane_head = _iota((HG_CHUNK, width), 1) // HEAD_DIM
    v_heads = [iv[:, hd * HEAD_DIM:(hd + 1) * HEAD_DIM] for hd in range(N_HEADS)]
    increments = []
    for i in range(n_chunks):
        sl = slice(i * HG_CHUNK, (i + 1) * HG_CHUNK)
        k_stack = jnp.concatenate([jnp.where(chunk_lane_head == hd, kd[sl], 0.0) for hd in range(N_HEADS)], axis=0)
        v_stack = jnp.concatenate([vh[sl] for vh in v_heads], axis=0)
        increments.append(_dot_tn(v_stack, k_stack))

    lane_head = _iota((ATT_BLOCK, width), 1) // HEAD_DIM
    mask_stack = jnp.concatenate([same_chunk_causal] * N_HEADS, axis=0)
    blocks = range(0, rows, ATT_BLOCK)
    scores = [_dot_nt(jnp.concatenate([jnp.where(lane_head == hd, qt[s:s + ATT_BLOCK], 0.0)
                                       for hd in range(N_HEADS)], axis=0), kt[s:s + ATT_BLOCK]) for s in blocks]
    work.step()

    state = st_ref[...]
    states = []
    for i in range(n_chunks):
        states.append(state)
        state = state * dec[i] + increments[i]
    st_ref[...] = state

    o_fulls = [_dot(jnp.where(mask_stack, sc, 0.0), iv[s:s + ATT_BLOCK]) for sc, s in zip(scores, blocks)]
    p_parts = []
    for i in range(n_chunks):
        sl = slice(i * HG_CHUNK, (i + 1) * HG_CHUNK)
        q_stack = jnp.concatenate([jnp.where(chunk_lane_head == hd, qd[sl], 0.0) for hd in range(N_HEADS)], axis=0)
        p_parts.append(_dot_nt(q_stack, states[i]))
    work.step()
    o_parts = []
    for o_full in o_fulls:
        o_blk = o_full[0:ATT_BLOCK]
        for hd in range(1, N_HEADS):
            o_blk = jnp.where(lane_head == hd, o_full[hd * ATT_BLOCK:(hd + 1) * ATT_BLOCK], o_blk)
        o_parts.append(o_blk)
    o_intra = jnp.concatenate(o_parts, axis=0)
    o_inter = jnp.concatenate(
        [jnp.concatenate([p[hd * HG_CHUNK:(hd + 1) * HG_CHUNK] for p in p_parts], axis=0) for hd in range(N_HEADS)],
        axis=1)
    o = o_intra + o_inter

    ms = _dot_split(o * o, _head_mean_matrix(width))
    o = (o * lax.rsqrt(ms + EPS)) * out_w
    return o * _sigmoid(g)


def _retention_block(q, k, v, g, gn_w, gn_b, st_ref, work):
    rows, width = q.shape
    cl = ATT_BLOCK
    lane_head_row = _iota((1, width), 1) // HEAD_DIM
    log_gamma_row = jnp.zeros((1, width), F32)
    for hd in range(N_HEADS):
        log_gamma_row = jnp.where(lane_head_row == hd, math.log1p(-2.0 ** (-5.0 - hd)), log_gamma_row)
    pos_col = _iota((cl, width), 0).astype(F32)
    rel = _iota((cl, cl), 0) - _iota((cl, cl), 1)
    rel_f = jnp.maximum(rel, 0).astype(F32)
    lane_head = _iota((cl, width), 1) // HEAD_DIM
    st_r = _iota((width, width), 0) // HEAD_DIM
    st_c = _iota((width, width), 1) // HEAD_DIM
    same_head = st_r == st_c
    q_in_decay = jnp.exp(log_gamma_row * (pos_col + 1.0))
    k_out_decay = jnp.exp(log_gamma_row * (cl - 1.0 - pos_col))
    chunk_decay = jnp.exp(log_gamma_row * float(cl))
    decay_stack = jnp.concatenate(
        [jnp.where(rel >= 0, jnp.exp(math.log1p(-2.0 ** (-5.0 - hd)) * rel_f), 0.0) for hd in range(N_HEADS)],
        axis=0)

    state = st_ref[...]
    o_parts = []
    for s in range(0, rows, cl):
        qs = q[s:s + cl] * HEAD_DIM ** -0.5
        kc, vc = k[s:s + cl], v[s:s + cl]
        o_blk = _dot_nt(qs * q_in_decay, state)
        q_stack = jnp.concatenate([jnp.where(lane_head == hd, qs, 0.0) for hd in range(N_HEADS)], axis=0)
        o_full = _dot(_dot_nt(q_stack, kc) * decay_stack, vc)
        for hd in range(N_HEADS):
            o_blk = o_blk + jnp.where(lane_head == hd, o_full[hd * cl:(hd + 1) * cl], 0.0)
        o_parts.append(o_blk)
        d_state = jnp.where(same_head, _dot_tn(vc, kc * k_out_decay), 0.0)
        state = state * chunk_decay + d_state
        if s == 0:
            work.step()
    st_ref[...] = state
    o = jnp.concatenate(o_parts, axis=0)

    mean_m = _head_mean_matrix(width)
    mu = _dot_split(o, mean_m)
    d = o - mu
    var = _dot_split(d * d, mean_m)
    work.step()
    o = d * lax.rsqrt(var + EPS) * gn_w + gn_b
    return o * (g * _sigmoid(g))


def _swa_block(q, k, v, qn_w, kn_w, sink_logit, kprev_ref, vprev_ref, block_index, work):
    rows = q.shape[0]
    kv_w = ATT_KV_HEADS * HEAD_DIM
    qn = (q * lax.rsqrt(_dot_split(q * q, _head_mean_matrix(q.shape[1])) + EPS)) * qn_w
    kn = (k * lax.rsqrt(_dot_split(k * k, _head_mean_matrix(kv_w)) + EPS)) * kn_w
    qb = ATT_BLOCK
    dist = _iota((qb, 2 * qb), 0) + qb - _iota((qb, 2 * qb), 1)
    in_window = (dist >= 0) & (dist < WINDOW)
    has_key = _iota((qb, 2 * qb), 1) + block_index * (2 * qb) >= qb
    dist_f = dist.astype(F32)
    lane_kv = _iota((qb, kv_w), 1) // HEAD_DIM
    n_blocks = rows // qb

    k_prev, v_prev = kprev_ref[...], vprev_ref[...]
    scores, values, sink_of = [], [], []
    for j in range(n_blocks):
        s = j * qb
        k_cur, v_cur = kn[s:s + qb], v[s:s + qb]
        kw = jnp.concatenate([k_prev, k_cur], axis=0)
        values.append(jnp.concatenate([v_prev, v_cur], axis=0))
        valid = in_window if j > 0 else in_window & has_key
        for kv in range(ATT_KV_HEADS):
            q_stack = jnp.concatenate(
                [jnp.where(lane_kv == kv, qn[s:s + qb, tile * kv_w:(tile + 1) * kv_w], 0.0) for tile in range(2)],
                axis=0)
            sc = _dot_nt(q_stack, kw) * HEAD_DIM ** -0.5
            for tile in range(2):
                head = kv * 2 + tile
                slope = 2.0 ** (-8.0 * (head + 1) / N_HEADS)
                scores.append(jnp.where(valid, sc[tile * qb:(tile + 1) * qb] - slope * dist_f, MASK_VALUE))
                sink_of.append(sink_logit(head))
        k_prev, v_prev = k_cur, v_cur
    work.step()
    kprev_ref[...] = k_prev
    vprev_ref[...] = v_prev

    maxes = [jnp.maximum(jnp.max(sc, axis=-1, keepdims=True), sink) for sc, sink in zip(scores, sink_of)]
    exps = [jnp.exp(sc - m) for sc, m in zip(scores, maxes)]
    work.step()
    probs = [e / (jnp.sum(e, axis=-1, keepdims=True) + jnp.exp(sink - m))
             for e, m, sink in zip(exps, maxes, sink_of)]

    o_parts = []
    for j in range(n_blocks):
        o_kv = [_dot(jnp.concatenate(probs[(j * ATT_KV_HEADS + kv) * 2:(j * ATT_KV_HEADS + kv + 1) * 2], axis=0),
                     values[j]) for kv in range(ATT_KV_HEADS)]
        o_tiles = [jnp.where(lane_kv == 0, o_kv[0][tile * qb:(tile + 1) * qb], o_kv[1][tile * qb:(tile + 1) * qb])
                   for tile in range(2)]
        o_parts.append(jnp.concatenate(o_tiles, axis=1))
        if j == 0:
            work.step()
    return jnp.concatenate(o_parts, axis=0)


def _rglru_block(xb, gb, conv_w, conv_b, wa, ba, wx, bx, lam, xpad_ref, h_ref, work):
    rows, width = xb.shape
    work.step()
    padded = jnp.concatenate([xpad_ref[...], xb], axis=0)
    xc = conv_b
    for j in range(CONV_WIDTH):
        back = CONV_WIDTH - 1 - j
        shifted = pltpu.roll(padded, back, axis=0) if back else padded
        xc = xc + shifted[SUBLANES:] * conv_w[j:j + 1, :]
    xpad_ref[...] = xb[rows - SUBLANES:rows, :]

    r_pre = _dot(xc, wa)
    i_pre = _dot(xc, wx)
    work.step()
    r = _sigmoid(r_pre + ba)
    i = _sigmoid(i_pre + bx)
    log_a = -LRU_C * r * _softplus(-lam)
    a = jnp.exp(log_a)
    u = jnp.sqrt(-jnp.tanh(log_a) * (a * a + 1.0)) * (i * xc)
    work.step()

    groups = rows // SUBLANES
    a3 = a.reshape(groups, SUBLANES, width)
    u3 = u.reshape(groups, SUBLANES, width)
    sub = _iota((groups, SUBLANES, width), 1)
    shift = 1
    while shift < SUBLANES:
        a_prev = pltpu.roll(a3, shift, axis=1)
        u_prev = pltpu.roll(u3, shift, axis=1)
        take = sub >= shift
        u3 = jnp.where(take, a3 * u_prev + u3, u3)
        a3 = jnp.where(take, a3 * a_prev, a3)
        shift *= 2
    carry = h_ref[...]
    work.step()
    h_groups = []
    for grp in range(groups):
        h_grp = u3[grp] + a3[grp] * carry
        h_groups.append(h_grp)
        carry = h_grp[SUBLANES - 1:SUBLANES, :]
    h_ref[...] = carry
    return jnp.concatenate(h_groups, axis=0) * jax.nn.gelu(gb)


def _mixer_body(layer, x_ref, lbl_ref, sinks_ref, nw_ref, win_ref, waq_ref, gbias_ref, hgw_ref, gnw_ref, gnb_ref,
                qnw_ref, knw_ref, cw_ref, cb_ref, wa_ref, ba_ref, wx_ref, bx_ref, lam_ref,
                wbr_ref, wout_ref, o_ref,
                hg_state, ret_state, kprev, vprev, xpad, lru_h, gates, ys, h_s, pj_lru, pj_swa, pj_hgrn, pj_ret):
    t = pl.program_id(1)

    @pl.when(t == 0)
    def _():
        hg_state[...] = jnp.zeros_like(hg_state)
        ret_state[...] = jnp.zeros_like(ret_state)
        kprev[...] = jnp.zeros_like(kprev)
        vprev[...] = jnp.zeros_like(vprev)
        xpad[...] = jnp.zeros_like(xpad)
        lru_h[...] = jnp.zeros_like(lru_h)

    d = x_ref.shape[1]
    h_s[...] = _rmsnorm_rows(x_ref[...], nw_ref[...]).astype(BF16)
    w = BRANCH_WIDTH
    kv_w = ATT_KV_HEADS * HEAD_DIM

    def proj(start, size):
        if (start, size) == (8 * w, w):
            weights = waq_ref[...]
        else:
            assert start + size <= 8 * w or start >= 9 * w
            weights = win_ref[:, start:start + size]
        return jnp.dot(h_s[...], weights, preferred_element_type=F32)

    logits = lbl_ref[...]
    e = jnp.exp(logits - jnp.max(logits, axis=0, keepdims=True))
    p = e / jnp.sum(e, axis=0, keepdims=True)
    cum = p[0:1, :]
    for j in range(1, layer + 1):
        cum = cum + p[j:j + 1, :]
    lb = cum - p[0:1, :]

    names = ("hq", "hf", "hi", "hg", "rq", "rk", "rv", "rg", "aq", "ak", "av", "lx", "lg")
    sizes = (w, w, w, w, w, w, w, w, w, kv_w, kv_w, w, w)
    offsets = {}
    off = 0
    for name, size in zip(names, sizes):
        offsets[name] = (off, size)
        off += size
    gate_off = off

    pj_of = {"lx": pj_lru, "aq": pj_swa, "hq": pj_hgrn, "rq": pj_ret}

    def proj_piece(ref, base, start, size):
        piece = proj(start, size)
        ref[:, start - base:start - base + size] = piece
        return piece[MIX_ROWS - SUBLANES:, size - 128:]

    def proj_pieces(*group):
        start, stop = offsets[group[0]][0], sum(offsets[group[-1]])
        cols = w if group[0] == "aq" else PROJ_COLS
        return [functools.partial(proj_piece, pj_of[group[0]], start, s, min(cols, stop - s))
                for s in range(start, stop, cols)]

    def projected(*group):
        base = offsets[group[0]][0]
        return [pj_of[group[0]][:, offsets[name][0] - base:sum(offsets[name]) - base] for name in group]

    def gate_chunk(c):
        cols = slice(c * GATE_COLS, (c + 1) * GATE_COLS)
        gate = _sigmoid(proj(gate_off + c * GATE_COLS, GATE_COLS) + gbias_ref[:, cols])
        gates[:, cols] = gate
        return gate[MIX_ROWS - SUBLANES:, GATE_COLS - 128:]

    gate_chunks = iter(range(N_BRANCH * d // GATE_COLS))

    def phase(index, n_gate_chunks, next_inputs, mixer):
        work = _WorkQueue()
        thunks = proj_pieces(*next_inputs) if next_inputs else []
        thunks += [functools.partial(gate_chunk, next(gate_chunks)) for _ in range(n_gate_chunks)]
        for thunk in thunks:
            work.add(thunk)

        @pl.when(t >= -index)
        def _():
            mixer(work)
            work.flush()

    def lru_phase(work):
        ys[3] = _rglru_block(*projected("lx", "lg"), cw_ref[...], cb_ref[...], wa_ref[...],
                             ba_ref[...], wx_ref[...], bx_ref[...], lam_ref[...], xpad, lru_h, work).astype(BF16)

    def swa_phase(work):
        ys[2] = _swa_block(*projected("aq", "ak", "av"),
                           qnw_ref[...], knw_ref[...], lambda head: sinks_ref[layer, head], kprev, vprev, t,
                           work).astype(BF16)

    def hgrn_phase(work):
        ys[0] = _hgrn2_block(*projected("hq", "hf", "hi", "hg"), lb, hgw_ref[...], hg_state, work).astype(BF16)

    def ret_phase(work):
        ys[1] = _retention_block(*projected("rq", "rk", "rv", "rg"), gnw_ref[...], gnb_ref[...], ret_state, work).astype(BF16)

    def merge_phase(work):
        work.flush()
        merged = jnp.zeros(x_ref.shape, F32)
        for n in range(N_BRANCH):
            merged = merged + gates[:, n * d:(n + 1) * d] * jnp.dot(ys[n], wbr_ref[n], preferred_element_type=F32)
        o_ref[...] = x_ref[...] + _dot(merged, wout_ref[...])

    for thunk in proj_pieces("lx", "lg"):
        thunk()
    phase(0, 3, ("aq", "ak", "av"), lru_phase)
    phase(1, 1, ("hq", "hf", "hi", "hg"), swa_phase)
    phase(2, 1, ("rq", "rk", "rv", "rg"), hgrn_phase)
    phase(3, 2, (), ret_phase)
    phase(4, 1, (), merge_phase)


def _mixer(x2, n_seq, layer, lb_logits, sinks, per_layer):
    m, d = x2.shape
    seq = m // n_seq
    assert seq % MIX_ROWS == 0 and MIX_ROWS % ATT_BLOCK == 0
    steps = seq // MIX_ROWS
    w = BRANCH_WIDTH
    kv_w = ATT_KV_HEADS * HEAD_DIM
    operands = [x2, lb_logits, sinks] + list(per_layer)
    in_specs = [pl.BlockSpec((MIX_ROWS, d), lambda b, t: (b * steps + t, 0)),
                pl.BlockSpec(lb_logits.shape, lambda *_: (0, 0), pipeline_mode=pl.Buffered(1)),
                pl.BlockSpec(memory_space=pltpu.SMEM)]
    in_specs += [_layer_spec(op, layer) for op in per_layer]
    return pl.pallas_call(
        functools.partial(_mixer_body, layer),
        grid=(n_seq, steps),
        in_specs=in_specs,
        out_specs=pl.BlockSpec((MIX_ROWS, d), lambda b, t: (b * steps + t, 0)),
        out_shape=jax.ShapeDtypeStruct((m, d), F32),
        scratch_shapes=[
            pltpu.VMEM((HEAD_DIM, w), F32),
            pltpu.VMEM((w, w), F32),
            pltpu.VMEM((ATT_BLOCK, kv_w), F32),
            pltpu.VMEM((ATT_BLOCK, kv_w), F32),
            pltpu.VMEM((SUBLANES, w), F32),
            pltpu.VMEM((1, w), F32),
            pltpu.VMEM((MIX_ROWS, N_BRANCH * d), F32),
            pltpu.VMEM((N_BRANCH, MIX_ROWS, w), BF16),
            pltpu.VMEM((MIX_ROWS, d), BF16),
            pltpu.VMEM((MIX_ROWS, 2 * w), F32),
            pltpu.VMEM((MIX_ROWS, w + 2 * kv_w), F32),
            pltpu.VMEM((MIX_ROWS, 4 * w), F32),
            pltpu.VMEM((MIX_ROWS, 4 * w), F32),
        ],
        compiler_params=pltpu.CompilerParams(
            dimension_semantics=("arbitrary", "arbitrary"), vmem_limit_bytes=VMEM_LIMIT_BYTES),
        name="token_mixer",
    )(*operands)


def _block_diag(blocks):
    layers, n, c, e = blocks.shape
    eye = jnp.eye(n, dtype=blocks.dtype)
    return (eye[None, :, None, :, None] * blocks[:, :, :, None, :]).reshape(layers, n * c, n * e)


def _swa_head_order(t, axis):
    parts = jnp.split(t, N_HEADS, axis=axis)
    return jnp.concatenate([parts[0], parts[2], parts[1], parts[3]], axis=axis)


def kernel(x, ffn1_norm, ffn1_wg, ffn1_wu, ffn1_wd, mix_norm, w_in, gate_bias, hgrn_lb_logits, hgrn_out_norm, ret_gn_w, ret_gn_b, attn_q_norm, attn_k_norm, attn_sinks, lru_conv_w, lru_conv_b, lru_wa, lru_ba, lru_wx, lru_bx, lru_lambda, w_branch, w_out, ffn2_norm, ffn2_wg, ffn2_wu, ffn2_wd):
    n_seq, seq, d = x.shape
    depth = w_in.shape[0]
    w = BRANCH_WIDTH
    aq = 8 * w
    ffn1 = (ffn1_norm, ffn1_wg.astype(BF16), ffn1_wu.astype(BF16), ffn1_wd.astype(BF16))
    ffn2 = (ffn2_norm, ffn2_wg.astype(BF16), ffn2_wu.astype(BF16), ffn2_wd.astype(BF16))
    w_branch_perm = jnp.concatenate(
        [w_branch[:, :2], _swa_head_order(w_branch[:, 2], 1)[:, None], w_branch[:, 3:]], axis=1)
    mixer_operands = (
        _rows(mix_norm), w_in.astype(BF16), _swa_head_order(w_in[:, :, aq:aq + w], 2).astype(BF16),
        _rows(gate_bias), _rows(hgrn_out_norm), _rows(ret_gn_w), _rows(ret_gn_b),
        _rows(jnp.tile(attn_q_norm, (1, N_HEADS))), _rows(jnp.tile(attn_k_norm, (1, ATT_KV_HEADS))),
        lru_conv_w, _rows(lru_conv_b), _block_diag(lru_wa).astype(BF16), _rows(lru_ba),
        _block_diag(lru_wx).astype(BF16), _rows(lru_bx), _rows(lru_lambda),
        w_branch_perm.astype(BF16), w_out.astype(BF16))
    x2 = x.reshape(n_seq * seq, d)
    for l in range(depth):
        x2 = _ffn(x2, l, *ffn1)
        x2 = _mixer(x2, n_seq, l, hgrn_lb_logits, attn_sinks, mixer_operands)
        x2 = _ffn(x2, l, *ffn2)
    return x2.reshape(n_seq, seq, d)
```

```python
import functools
import math

import jax
import jax.numpy as jnp
from jax import lax
from jax.experimental import pallas as pl
from jax.experimental.pallas import tpu as pltpu

HEAD_DIM = 64
BRANCH_WIDTH = 256
N_BRANCH = 4
N_HEADS = 4
HG_CHUNK = 16
LB_FLOOR = 1e-30
ATT_KV_HEADS = 2
ATT_BLOCK = 128
WINDOW = 128
MASK_VALUE = -1e30
CONV_WIDTH = 4
LRU_C = 8.0
EPS = 1e-6

V7X_VMEM_BYTES = 64 * 1024 * 1024
VMEM_LIMIT_BYTES = V7X_VMEM_BYTES - 8 * 1024 * 1024
SUBLANES = 8
LANES = 128

FFN_ROWS = 1024
FFN_COLS = 256
MIX_ROWS = 512
GATE_COLS = 512
PROJ_COLS = 512

BF16 = jnp.bfloat16
F32 = jnp.float32


def _dot(a, b):
    return jnp.dot(a.astype(BF16), b.astype(BF16), preferred_element_type=F32)


def _dot_nt(a, b):
    return lax.dot_general(a.astype(BF16), b.astype(BF16), (((1,), (1,)), ((), ())),
                           preferred_element_type=F32)


def _dot_tn(a, b):
    return lax.dot_general(a.astype(BF16), b.astype(BF16), (((0,), (0,)), ((), ())),
                           preferred_element_type=F32)


def _rmsnorm_rows(x, w_row):
    return (x * lax.rsqrt(jnp.mean(x * x, axis=-1, keepdims=True) + EPS)) * w_row


def _softplus(x):
    return jnp.maximum(x, 0.0) + jnp.log1p(jnp.exp(-jnp.abs(x)))


def _sigmoid(x):
    return 0.5 * jnp.tanh(0.5 * x) + 0.5


def _iota(shape, axis):
    return lax.broadcasted_iota(jnp.int32, shape, axis)


def _matmul_weight(w):
    w = w.astype(BF16)
    if (w.shape[-1] // LANES) % 4:
        return w
    return jnp.pad(w, [(0, 0)] * (w.ndim - 1) + [(0, LANES)])


def _ffn_body(x_ref, nw_ref, wg_ref, wu_ref, wd_ref, o_ref):
    x = x_ref[...]
    d = x.shape[1]
    h = _rmsnorm_rows(x, nw_ref[...]).astype(BF16)
    d_ff = wd_ref.shape[0]
    acc = jnp.zeros(x.shape, F32)
    for c in range(d_ff // FFN_COLS):
        cols = slice(c * FFN_COLS, (c + 1) * FFN_COLS)
        g = jnp.dot(h, wg_ref[:, cols], preferred_element_type=F32)
        u = jnp.dot(h, wu_ref[:, cols], preferred_element_type=F32)
        a = (jax.nn.silu(g) * u).astype(BF16)
        acc = acc + jnp.dot(a, wd_ref[cols, 0:d], preferred_element_type=F32)
    o_ref[...] = x + 0.5 * acc


def _layer_spec(stacked, layer):
    index = (layer,) + (0,) * (stacked.ndim - 1)
    return pl.BlockSpec((None,) + stacked.shape[1:], lambda *_: index, pipeline_mode=pl.Buffered(1))


def _rows(stacked):
    return stacked.reshape(stacked.shape[0], 1, -1)


def _ffn(x2, layer, norm_w, wg, wu, wd):
    m, d = x2.shape
    d_ff = wd.shape[1]
    assert m % FFN_ROWS == 0 and d_ff % FFN_COLS == 0
    operands = (_rows(norm_w), wg, wu, wd)
    return pl.pallas_call(
        _ffn_body,
        grid=(m // FFN_ROWS,),
        in_specs=[pl.BlockSpec((FFN_ROWS, d), lambda i: (i, 0))] + [_layer_spec(op, layer) for op in operands],
        out_specs=pl.BlockSpec((FFN_ROWS, d), lambda i: (i, 0)),
        out_shape=jax.ShapeDtypeStruct((m, d), F32),
        compiler_params=pltpu.CompilerParams(
            dimension_semantics=("arbitrary",), vmem_limit_bytes=VMEM_LIMIT_BYTES),
        name="channel_mixer",
    )(x2, *operands)


class _WorkQueue:
    def __init__(self):
        self._pending = []

    def add(self, thunk):
        self._pending.append(thunk)

    def step(self):
        if self._pending:
            self._pending.pop(0)()

    def flush(self):
        while self._pending:
            self.step()


def _head_mean_matrix(width):
    r = _iota((width, width), 0) // HEAD_DIM
    c = _iota((width, width), 1) // HEAD_DIM
    return jnp.where(r == c, 1.0 / HEAD_DIM, 0.0).astype(BF16)


def _hgrn2_block(q, z, iv, g, lb, out_w, st_ref, work):
    rows, width = q.shape
    n_chunks = rows // HG_CHUNK
    work.step()
    log_lb = jnp.log(jnp.maximum(lb, LB_FLOOR))
    log_sig = -_softplus(-z)
    c = jnp.log1p(-lb) + log_sig
    log_f = jnp.maximum(log_lb, c) + jnp.log1p(jnp.exp(-jnp.abs(log_lb - c)))
    k = (1.0 - lb) * _sigmoid(-z)

    r_i = _iota((ATT_BLOCK, ATT_BLOCK), 0)
    c_i = _iota((ATT_BLOCK, ATT_BLOCK), 1)
    same_chunk_causal = (r_i // HG_CHUNK == c_i // HG_CHUNK) & (c_i <= r_i)
    tri = jnp.where(same_chunk_causal, 1.0, 0.0).astype(BF16)
    lf_hi = log_f.astype(BF16)
    lf_lo = (log_f - lf_hi.astype(F32)).astype(BF16)
    b = jnp.concatenate(
        [jnp.dot(tri, lf_hi[s:s + ATT_BLOCK], preferred_element_type=F32)
         + jnp.dot(tri, lf_lo[s:s + ATT_BLOCK], preferred_element_type=F32)
         for s in range(0, rows, ATT_BLOCK)], axis=0)
    work.step()

    b3 = b.reshape(n_chunks, HG_CHUNK, width)
    b_ref = b3[:, HG_CHUNK // 2:HG_CHUNK // 2 + 1, :]
    b_last = b3[:, HG_CHUNK - 1:HG_CHUNK, :]
    q3 = q.reshape(n_chunks, HG_CHUNK, width)
    k3 = k.reshape(n_chunks, HG_CHUNK, width)
    qt = (q3 * jnp.exp(b3 - b_ref)).reshape(rows, width)
    kt = (k3 * jnp.exp(b_ref - b3)).reshape(rows, width)
    qd = (q3 * jnp.exp(b3)).reshape(rows, width)
    kd = (k3 * jnp.exp(b_last - b3)).reshape(rows, width)
    dec = jnp.exp(b_last)

    chunk_lane_head = _iota((HG_CHUNK, width), 1) // HEAD_DIM
    v_heads = [iv[:, hd * HEAD_DIM:(hd + 1) * HEAD_DIM] for hd in range(N_HEADS)]
    increments = []
    for i in range(n_chunks):
        sl = slice(i * HG_CHUNK, (i + 1) * HG_CHUNK)
        k_stack = jnp.concatenate([jnp.where(chunk_lane_head == hd, kd[sl], 0.0) for hd in range(N_HEADS)], axis=0)
        v_stack = jnp.concatenate([vh[sl] for vh in v_heads], axis=0)
        increments.append(_dot_tn(v_stack, k_stack))

    lane_head = _iota((ATT_BLOCK, width), 1) // HEAD_DIM
    mask_stack = jnp.concatenate([same_chunk_causal] * N_HEADS, axis=0)
    blocks = range(0, rows, ATT_BLOCK)
    scores = [_dot_nt(jnp.concatenate([jnp.where(lane_head == hd, qt[s:s + ATT_BLOCK], 0.0)
                                       for hd in range(N_HEADS)], axis=0), kt[s:s + ATT_BLOCK]) for s in blocks]
    work.step()

    state = st_ref[...]
    states = []
    for i in range(n_chunks):
        states.append(state)
        state = state * dec[i] + increments[i]
    st_ref[...] = state

    o_fulls = [_dot(jnp.where(mask_stack, sc, 0.0), iv[s:s + ATT_BLOCK]) for sc, s in zip(scores, blocks)]
    p_parts = []
    for i in range(n_chunks):
        sl = slice(i * HG_CHUNK, (i + 1) * HG_CHUNK)
        q_stack = jnp.concatenate([jnp.where(chunk_lane_head == hd, qd[sl], 0.0) for hd in range(N_HEADS)], axis=0)
        p_parts.append(_dot_nt(q_stack, states[i]))
    work.step()
    o_parts = []
    for o_full in o_fulls:
        o_blk = o_full[0:ATT_BLOCK]
        for hd in range(1, N_HEADS):
            o_blk = jnp.where(lane_head == hd, o_full[hd * ATT_BLOCK:(hd + 1) * ATT_BLOCK], o_blk)
        o_parts.append(o_blk)
    o_intra = jnp.concatenate(o_parts, axis=0)
    o_inter = jnp.concatenate(
        [jnp.concatenate([p[hd * HG_CHUNK:(hd + 1) * HG_CHUNK] for p in p_parts], axis=0) for hd in range(N_HEADS)],
        axis=1)
    o = o_intra + o_inter

    ms = _dot(o * o, _head_mean_matrix(width))
    o = (o * lax.rsqrt(ms + EPS)) * out_w
    return o * _sigmoid(g)


def _retention_block(q, k, v, g, gn_w, gn_b, st_ref, work):
    rows, width = q.shape
    cl = ATT_BLOCK
    lane_head_row = _iota((1, width), 1) // HEAD_DIM
    log_gamma_row = jnp.zeros((1, width), F32)
    for hd in range(N_HEADS):
        log_gamma_row = jnp.where(lane_head_row == hd, math.log1p(-2.0 ** (-5.0 - hd)), log_gamma_row)
    pos_col = _iota((cl, width), 0).astype(F32)
    rel = _iota((cl, cl), 0) - _iota((cl, cl), 1)
    rel_f = jnp.maximum(rel, 0).astype(F32)
    lane_head = _iota((cl, width), 1) // HEAD_DIM
    st_r = _iota((width, width), 0) // HEAD_DIM
    st_c = _iota((width, width), 1) // HEAD_DIM
    same_head = st_r == st_c
    q_in_decay = jnp.exp(log_gamma_row * (pos_col + 1.0))
    k_out_decay = jnp.exp(log_gamma_row * (cl - 1.0 - pos_col))
    chunk_decay = jnp.exp(log_gamma_row * float(cl))
    decay_stack = jnp.concatenate(
        [jnp.where(rel >= 0, jnp.exp(math.log1p(-2.0 ** (-5.0 - hd)) * rel_f), 0.0) for hd in range(N_HEADS)],
        axis=0)

    state = st_ref[...]
    o_parts = []
    for s in range(0, rows, cl):
        qs = q[s:s + cl] * HEAD_DIM ** -0.5
        kc, vc = k[s:s + cl], v[s:s + cl]
        o_blk = _dot_nt(qs * q_in_decay, state)
        q_stack = jnp.concatenate([jnp.where(lane_head == hd, qs, 0.0) for hd in range(N_HEADS)], axis=0)
        o_full = _dot(_dot_nt(q_stack, kc) * decay_stack, vc)
        for hd in range(N_HEADS):
            o_blk = o_blk + jnp.where(lane_head == hd, o_full[hd * cl:(hd + 1) * cl], 0.0)
        o_parts.append(o_blk)
        d_state = jnp.where(same_head, _dot_tn(vc, kc * k_out_decay), 0.0)
        state = state * chunk_decay + d_state
        if s == 0:
            work.step()
    st_ref[...] = state
    o = jnp.concatenate(o_parts, axis=0)

    mean_m = _head_mean_matrix(width)
    mu = _dot(o, mean_m)
    d = o - mu
    var = _dot(d * d, mean_m)
    work.step()
    o = d * lax.rsqrt(var + EPS) * gn_w + gn_b
    return o * (g * _sigmoid(g))


def _swa_block(q, k, v, qn_w, kn_w, sink_logit, kprev_ref, vprev_ref, block_index, work):
    rows = q.shape[0]
    kv_w = ATT_KV_HEADS * HEAD_DIM
    qn = (q * lax.rsqrt(_dot(q * q, _head_mean_matrix(q.shape[1])) + EPS)) * qn_w
    kn = (k * lax.rsqrt(_dot(k * k, _head_mean_matrix(kv_w)) + EPS)) * kn_w
    qb = ATT_BLOCK
    dist = _iota((qb, 2 * qb), 0) + qb - _iota((qb, 2 * qb), 1)
    in_window = (dist >= 0) & (dist < WINDOW)
    has_key = _iota((qb, 2 * qb), 1) + block_index * (2 * qb) >= qb
    dist_f = dist.astype(F32)
    lane_kv = _iota((qb, kv_w), 1) // HEAD_DIM
    n_blocks = rows // qb

    k_prev, v_prev = kprev_ref[...], vprev_ref[...]
    scores, values, sink_of = [], [], []
    for j in range(n_blocks):
        s = j * qb
        k_cur, v_cur = kn[s:s + qb], v[s:s + qb]
        kw = jnp.concatenate([k_prev, k_cur], axis=0)
        values.append(jnp.concatenate([v_prev, v_cur], axis=0))
        valid = in_window if j > 0 else in_window & has_key
        for kv in range(ATT_KV_HEADS):
            q_stack = jnp.concatenate(
                [jnp.where(lane_kv == kv, qn[s:s + qb, tile * kv_w:(tile + 1) * kv_w], 0.0) for tile in range(2)],
                axis=0)
            sc = _dot_nt(q_stack, kw) * HEAD_DIM ** -0.5
            for tile in range(2):
                head = kv * 2 + tile
                slope = 2.0 ** (-8.0 * (head + 1) / N_HEADS)
                scores.append(jnp.where(valid, sc[tile * qb:(tile + 1) * qb] - slope * dist_f, MASK_VALUE))
                sink_of.append(sink_logit(head))
        k_prev, v_prev = k_cur, v_cur
    work.step()
    kprev_ref[...] = k_prev
    vprev_ref[...] = v_prev

    maxes = [jnp.maximum(jnp.max(sc, axis=-1, keepdims=True), sink) for sc, sink in zip(scores, sink_of)]
    exps = [jnp.exp(sc - m) for sc, m in zip(scores, maxes)]
    work.step()
    probs = [e / (jnp.sum(e, axis=-1, keepdims=True) + jnp.exp(sink - m))
             for e, m, sink in zip(exps, maxes, sink_of)]

    o_parts = []
    for j in range(n_blocks):
        o_kv = [_dot(jnp.concatenate(probs[(j * ATT_KV_HEADS + kv) * 2:(j * ATT_KV_HEADS + kv + 1) * 2], axis=0),
                     values[j]) for kv in range(ATT_KV_HEADS)]
        o_tiles = [jnp.where(lane_kv == 0, o_kv[0][tile * qb:(tile + 1) * qb], o_kv[1][tile * qb:(tile + 1) * qb])
                   for tile in range(2)]
        o_parts.append(jnp.concatenate(o_tiles, axis=1))
        if j == 0:
            work.step()
    return jnp.concatenate(o_parts, axis=0)


def _rglru_block(xb, gb, conv_w, conv_b, wa, ba, wx, bx, lam, xpad_ref, h_ref, work):
    rows, width = xb.shape
    work.step()
    padded = jnp.concatenate([xpad_ref[...], xb], axis=0)
    xc = conv_b
    for j in range(CONV_WIDTH):
        back = CONV_WIDTH - 1 - j
        shifted = pltpu.roll(padded, back, axis=0) if back else padded
        xc = xc + shifted[SUBLANES:] * conv_w[j:j + 1, :]
    xpad_ref[...] = xb[rows - SUBLANES:rows, :]

    r_pre = _dot(xc, wa)
    i_pre = _dot(xc, wx)
    work.step()
    r = _sigmoid(r_pre + ba)
    i = _sigmoid(i_pre + bx)
    log_a = -LRU_C * r * _softplus(-lam)
    a = jnp.exp(log_a)
    u = jnp.sqrt(-jnp.tanh(log_a) * (a * a + 1.0)) * (i * xc)
    work.step()

    groups = rows // SUBLANES
    a3 = a.reshape(groups, SUBLANES, width)
    u3 = u.reshape(groups, SUBLANES, width)
    sub = _iota((groups, SUBLANES, width), 1)
    shift = 1
    while shift < SUBLANES:
        a_prev = pltpu.roll(a3, shift, axis=1)
        u_prev = pltpu.roll(u3, shift, axis=1)
        take = sub >= shift
        u3 = jnp.where(take, a3 * u_prev + u3, u3)
        a3 = jnp.where(take, a3 * a_prev, a3)
        shift *= 2
    carry = h_ref[...]
    work.step()
    h_groups = []
    for grp in range(groups):
        h_grp = u3[grp] + a3[grp] * carry
        h_groups.append(h_grp)
        carry = h_grp[SUBLANES - 1:SUBLANES, :]
    h_ref[...] = carry
    return jnp.concatenate(h_groups, axis=0) * jax.nn.gelu(gb)


def _mixer_body(layer, x_ref, lbl_ref, sinks_ref, nw_ref, win_ref, waq_ref, gbias_ref, hgw_ref, gnw_ref, gnb_ref,
                qnw_ref, knw_ref, cw_ref, cb_ref, wa_ref, ba_ref, wx_ref, bx_ref, lam_ref,
                wbr_ref, wout_ref, o_ref,
                hg_state, ret_state, kprev, vprev, xpad, lru_h, gates, ys, h_s, pj_lru, pj_swa, pj_hgrn, pj_ret):
    t = pl.program_id(1)

    @pl.when(t == 0)
    def _():
        hg_state[...] = jnp.zeros_like(hg_state)
        ret_state[...] = jnp.zeros_like(ret_state)
        kprev[...] = jnp.zeros_like(kprev)
        vprev[...] = jnp.zeros_like(vprev)
        xpad[...] = jnp.zeros_like(xpad)
        lru_h[...] = jnp.zeros_like(lru_h)

    d = x_ref.shape[1]
    w = BRANCH_WIDTH
    kv_w = ATT_KV_HEADS * HEAD_DIM

    def proj(start, size):
        if (start, size) == (8 * w, w):
            weights = waq_ref[:, 0:w]
        else:
            assert start + size <= 8 * w or start >= 9 * w
            weights = win_ref[:, start:start + size]
        return jnp.dot(h_s[...], weights, preferred_element_type=F32)

    logits = lbl_ref[...]
    e = jnp.exp(logits - jnp.max(logits, axis=0, keepdims=True))
    p = e / jnp.sum(e, axis=0, keepdims=True)
    cum = p[0:1, :]
    for j in range(1, layer + 1):
        cum = cum + p[j:j + 1, :]
    lb = cum - p[0:1, :]

    names = ("hq", "hf", "hi", "hg", "rq", "rk", "rv", "rg", "aq", "ak", "av", "lx", "lg")
    sizes = (w, w, w, w, w, w, w, w, w, kv_w, kv_w, w, w)
    offsets = {}
    off = 0
    for name, size in zip(names, sizes):
        offsets[name] = (off, size)
        off += size
    gate_off = off

    pj_of = {"lx": pj_lru, "aq": pj_swa, "hq": pj_hgrn, "rq": pj_ret}

    def proj_piece(ref, base, start, size):
        ref[:, start - base:start - base + size] = proj(start, size)

    def proj_pieces(*group):
        start, stop = offsets[group[0]][0], sum(offsets[group[-1]])
        cols = w if group[0] == "aq" else PROJ_COLS
        return [functools.partial(proj_piece, pj_of[group[0]], start, s, min(cols, stop - s))
                for s in range(start, stop, cols)]

    def projected(*group):
        base = offsets[group[0]][0]
        return [pj_of[group[0]][:, offsets[name][0] - base:sum(offsets[name]) - base] for name in group]

    def gate_chunk(c):
        cols = slice(c * GATE_COLS, (c + 1) * GATE_COLS)
        gates[:, cols] = _sigmoid(proj(gate_off + c * GATE_COLS, GATE_COLS) + gbias_ref[:, cols])

    gate_chunks = iter(range(N_BRANCH * d // GATE_COLS))

    def phase(index, n_gate_chunks, next_inputs, mixer):
        work = _WorkQueue()
        thunks = proj_pieces(*next_inputs) if next_inputs else []
        thunks += [functools.partial(gate_chunk, next(gate_chunks)) for _ in range(n_gate_chunks)]
        for thunk in thunks:
            work.add(thunk)

        @pl.when(t >= -index)
        def _():
            mixer(work)
            work.flush()

    def lru_phase(work):
        ys[3] = _rglru_block(*projected("lx", "lg"), cw_ref[...], cb_ref[...], wa_ref[:, 0:w],
                             ba_ref[...], wx_ref[:, 0:w], bx_ref[...], lam_ref[...], xpad, lru_h, work).astype(BF16)

    def swa_phase(work):
        ys[2] = _swa_block(*projected("aq", "ak", "av"),
                           qnw_ref[...], knw_ref[...], lambda head: sinks_ref[layer, head], kprev, vprev, t,
                           work).astype(BF16)

    def hgrn_phase(work):
        ys[0] = _hgrn2_block(*projected("hq", "hf", "hi", "hg"), lb, hgw_ref[...], hg_state, work).astype(BF16)

    def ret_phase(work):
        ys[1] = _retention_block(*projected("rq", "rk", "rv", "rg"), gnw_ref[...], gnb_ref[...], ret_state,
                                 work).astype(BF16)

    def merge_phase(work):
        work.flush()
        merged = jnp.zeros(x_ref.shape, F32)
        for n in range(N_BRANCH):
            merged = merged + gates[:, n * d:(n + 1) * d] * jnp.dot(ys[n], wbr_ref[n, :, 0:d],
                                                                    preferred_element_type=F32)
        o_ref[...] = x_ref[...] + _dot(merged, wout_ref[:, 0:d])

    h_s[...] = _rmsnorm_rows(x_ref[...], nw_ref[...]).astype(BF16)
    for thunk in proj_pieces("lx", "lg"):
        thunk()
    phase(0, 3, ("aq", "ak", "av"), lru_phase)
    phase(1, 1, ("hq", "hf", "hi", "hg"), swa_phase)
    phase(2, 1, ("rq", "rk", "rv", "rg"), hgrn_phase)
    phase(3, 2, (), ret_phase)
    phase(4, 1, (), merge_phase)


def _mixer(x2, n_seq, layer, lb_logits, sinks, per_layer):
    m, d = x2.shape
    seq = m // n_seq
    assert seq % MIX_ROWS == 0 and MIX_ROWS % ATT_BLOCK == 0
    steps = seq // MIX_ROWS
    w = BRANCH_WIDTH
    kv_w = ATT_KV_HEADS * HEAD_DIM
    operands = [x2, lb_logits, sinks] + list(per_layer)
    in_specs = [pl.BlockSpec((MIX_ROWS, d), lambda b, t: (b * steps + t, 0)),
                pl.BlockSpec(lb_logits.shape, lambda *_: (0, 0), pipeline_mode=pl.Buffered(1)),
                pl.BlockSpec(memory_space=pltpu.SMEM)]
    in_specs += [_layer_spec(op, layer) for op in per_layer]
    return pl.pallas_call(
        functools.partial(_mixer_body, layer),
        grid=(n_seq, steps),
        in_specs=in_specs,
        out_specs=pl.BlockSpec((MIX_ROWS, d), lambda b, t: (b * steps + t, 0)),
        out_shape=jax.ShapeDtypeStruct((m, d), F32),
        scratch_shapes=[
            pltpu.VMEM((HEAD_DIM, w), F32),
            pltpu.VMEM((w, w), F32),
            pltpu.VMEM((ATT_BLOCK, kv_w), F32),
            pltpu.VMEM((ATT_BLOCK, kv_w), F32),
            pltpu.VMEM((SUBLANES, w), F32),
            pltpu.VMEM((1, w), F32),
            pltpu.VMEM((MIX_ROWS, N_BRANCH * d), F32),
            pltpu.VMEM((N_BRANCH, MIX_ROWS, w), BF16),
            pltpu.VMEM((MIX_ROWS, d), BF16),
            pltpu.VMEM((MIX_ROWS, 2 * w), F32),
            pltpu.VMEM((MIX_ROWS, w + 2 * kv_w), F32),
            pltpu.VMEM((MIX_ROWS, 4 * w), F32),
            pltpu.VMEM((MIX_ROWS, 4 * w), F32),
        ],
        compiler_params=pltpu.CompilerParams(
            dimension_semantics=("arbitrary", "arbitrary"), vmem_limit_bytes=VMEM_LIMIT_BYTES),
        name="token_mixer",
    )(*operands)


def _block_diag(blocks):
    layers, n, c, e = blocks.shape
    eye = jnp.eye(n, dtype=blocks.dtype)
    return (eye[None, :, None, :, None] * blocks[:, :, :, None, :]).reshape(layers, n * c, n * e)


def _swa_head_order(t, axis):
    parts = jnp.split(t, N_HEADS, axis=axis)
    return jnp.concatenate([parts[0], parts[2], parts[1], parts[3]], axis=axis)


def kernel(x, ffn1_norm, ffn1_wg, ffn1_wu, ffn1_wd, mix_norm, w_in, gate_bias, hgrn_lb_logits, hgrn_out_norm, ret_gn_w, ret_gn_b, attn_q_norm, attn_k_norm, attn_sinks, lru_conv_w, lru_conv_b, lru_wa, lru_ba, lru_wx, lru_bx, lru_lambda, w_branch, w_out, ffn2_norm, ffn2_wg, ffn2_wu, ffn2_wd):
    n_seq, seq, d = x.shape
    depth = w_in.shape[0]
    w = BRANCH_WIDTH
    aq = 8 * w
    mw = _matmul_weight
    ffn1 = (ffn1_norm, mw(ffn1_wg), mw(ffn1_wu), mw(ffn1_wd))
    ffn2 = (ffn2_norm, mw(ffn2_wg), mw(ffn2_wu), mw(ffn2_wd))
    w_branch_perm = jnp.concatenate(
        [w_branch[:, :2], _swa_head_order(w_branch[:, 2], 1)[:, None], w_branch[:, 3:]], axis=1)
    mixer_operands = (
        _rows(mix_norm), mw(w_in), mw(_swa_head_order(w_in[:, :, aq:aq + w], 2)),
        _rows(gate_bias), _rows(hgrn_out_norm), _rows(ret_gn_w), _rows(ret_gn_b),
        _rows(jnp.tile(attn_q_norm, (1, N_HEADS))), _rows(jnp.tile(attn_k_norm, (1, ATT_KV_HEADS))),
        lru_conv_w, _rows(lru_conv_b), mw(_block_diag(lru_wa)), _rows(lru_ba),
        mw(_block_diag(lru_wx)), _rows(lru_bx), _rows(lru_lambda),
        mw(w_branch_perm), mw(w_out))
    x2 = x.reshape(n_seq * seq, d)
    for l in range(depth):
        x2 = _ffn(x2, l, *ffn1)
        x2 = _mixer(x2, n_seq, l, hgrn_lb_logits, attn_sinks, mixer_operands)
        x2 = _ffn(x2, l, *ffn2)
    return x2.reshape(n_seq, seq, d)
```

```python
import functools
import math

import jax
import jax.numpy as jnp
from jax import lax
from jax.experimental import pallas as pl
from jax.experimental.pallas import tpu as pltpu

HEAD_DIM = 64
BRANCH_WIDTH = 256
N_BRANCH = 4
N_HEADS = 4
HG_CHUNK = 16
LB_FLOOR = 1e-30
ATT_KV_HEADS = 2
ATT_BLOCK = 128
WINDOW = 128
MASK_VALUE = -1e30
CONV_WIDTH = 4
LRU_C = 8.0
EPS = 1e-6

V7X_VMEM_BYTES = 64 * 1024 * 1024
VMEM_LIMIT_BYTES = V7X_VMEM_BYTES - 8 * 1024 * 1024
SUBLANES = 8
LANES = 128

FFN_ROWS = 1024
FFN_COLS = 256
MIX_ROWS = 512
GATE_COLS = 512
PROJ_COLS = 512

BF16 = jnp.bfloat16
F32 = jnp.float32


def _dot(a, b):
    return jnp.dot(a.astype(BF16), b.astype(BF16), preferred_element_type=F32)


def _dot_nt(a, b):
    return lax.dot_general(a.astype(BF16), b.astype(BF16), (((1,), (1,)), ((), ())),
                           preferred_element_type=F32)


def _dot_tn(a, b):
    return lax.dot_general(a.astype(BF16), b.astype(BF16), (((0,), (0,)), ((), ())),
                           preferred_element_type=F32)


def _rmsnorm_rows(x, w_row):
    return (x * lax.rsqrt(jnp.mean(x * x, axis=-1, keepdims=True) + EPS)) * w_row


def _softplus(x):
    return jnp.maximum(x, 0.0) + jnp.log1p(jnp.exp(-jnp.abs(x)))


def _sigmoid(x):
    return 0.5 * jnp.tanh(0.5 * x) + 0.5


def _iota(shape, axis):
    return lax.broadcasted_iota(jnp.int32, shape, axis)


def _matmul_weight(w):
    if (w.shape[-1] // LANES) % 4 == 0:
        w = jnp.pad(w, [(0, 0)] * (w.ndim - 1) + [(0, LANES)])
    return w.astype(BF16)


def _ffn_body(x_ref, nw_ref, wg_ref, wu_ref, wd_ref, o_ref):
    x = x_ref[...]
    d = x.shape[1]
    h = _rmsnorm_rows(x, nw_ref[...]).astype(BF16)
    d_ff = wd_ref.shape[0]
    acc = jnp.zeros(x.shape, F32)
    for c in range(d_ff // FFN_COLS):
        cols = slice(c * FFN_COLS, (c + 1) * FFN_COLS)
        g = jnp.dot(h, wg_ref[:, cols], preferred_element_type=F32)
        u = jnp.dot(h, wu_ref[:, cols], preferred_element_type=F32)
        a = (jax.nn.silu(g) * u).astype(BF16)
        acc = acc + jnp.dot(a, wd_ref[cols, 0:d], preferred_element_type=F32)
    o_ref[...] = x + 0.5 * acc


def _layer_spec(stacked, layer):
    index = (layer,) + (0,) * (stacked.ndim - 1)
    return pl.BlockSpec((None,) + stacked.shape[1:], lambda *_: index, pipeline_mode=pl.Buffered(1))


def _rows(stacked):
    return stacked.reshape(stacked.shape[0], 1, -1)


def _ffn(x2, layer, norm_w, wg, wu, wd):
    m, d = x2.shape
    d_ff = wd.shape[1]
    assert m % FFN_ROWS == 0 and d_ff % FFN_COLS == 0
    operands = (_rows(norm_w), wg, wu, wd)
    return pl.pallas_call(
        _ffn_body,
        grid=(m // FFN_ROWS,),
        in_specs=[pl.BlockSpec((FFN_ROWS, d), lambda i: (i, 0))] + [_layer_spec(op, layer) for op in operands],
        out_specs=pl.BlockSpec((FFN_ROWS, d), lambda i: (i, 0)),
        out_shape=jax.ShapeDtypeStruct((m, d), F32),
        compiler_params=pltpu.CompilerParams(
            dimension_semantics=("arbitrary",), vmem_limit_bytes=VMEM_LIMIT_BYTES),
        name="channel_mixer",
    )(x2, *operands)


class _WorkQueue:
    def __init__(self):
        self._pending = []
        self.issued = 0

    def add(self, thunk):
        self._pending.append(thunk)

    def step(self):
        if self._pending:
            self._pending.pop(0)()
            self.issued += 1

    def flush(self):
        while self._pending:
            self.step()


def _head_mean_matrix(width):
    r = _iota((width, width), 0) // HEAD_DIM
    c = _iota((width, width), 1) // HEAD_DIM
    return jnp.where(r == c, 1.0 / HEAD_DIM, 0.0).astype(BF16)


def _hgrn2_block(q, z, iv, g, lb, out_w, st_ref, work):
    rows, width = q.shape
    n_chunks = rows // HG_CHUNK
    work.step()
    log_lb = jnp.log(jnp.maximum(lb, LB_FLOOR))
    log_sig = -_softplus(-z)
    c = jnp.log1p(-lb) + log_sig
    log_f = jnp.maximum(log_lb, c) + jnp.log1p(jnp.exp(-jnp.abs(log_lb - c)))
    k = (1.0 - lb) * _sigmoid(-z)

    r_i = _iota((ATT_BLOCK, ATT_BLOCK), 0)
    c_i = _iota((ATT_BLOCK, ATT_BLOCK), 1)
    same_chunk_causal = (r_i // HG_CHUNK == c_i // HG_CHUNK) & (c_i <= r_i)
    tri = jnp.where(same_chunk_causal, 1.0, 0.0).astype(BF16)
    lf_hi = log_f.astype(BF16)
    lf_lo = (log_f - lf_hi.astype(F32)).astype(BF16)
    b = jnp.concatenate(
        [jnp.dot(tri, lf_hi[s:s + ATT_BLOCK], preferred_element_type=F32)
         + jnp.dot(tri, lf_lo[s:s + ATT_BLOCK], preferred_element_type=F32)
         for s in range(0, rows, ATT_BLOCK)], axis=0)
    work.step()

    b3 = b.reshape(n_chunks, HG_CHUNK, width)
    b_ref = b3[:, HG_CHUNK // 2:HG_CHUNK // 2 + 1, :]
    b_last = b3[:, HG_CHUNK - 1:HG_CHUNK, :]
    q3 = q.reshape(n_chunks, HG_CHUNK, width)
    k3 = k.reshape(n_chunks, HG_CHUNK, width)
    qt = (q3 * jnp.exp(b3 - b_ref)).reshape(rows, width)
    kt = (k3 * jnp.exp(b_ref - b3)).reshape(rows, width)
    qd = (q3 * jnp.exp(b3)).reshape(rows, width)
    kd = (k3 * jnp.exp(b_last - b3)).reshape(rows, width)
    dec = jnp.exp(b_last)

    chunk_lane_head = _iota((HG_CHUNK, width), 1) // HEAD_DIM
    v_heads = [iv[:, hd * HEAD_DIM:(hd + 1) * HEAD_DIM] for hd in range(N_HEADS)]
    increments = []
    for i in range(n_chunks):
        sl = slice(i * HG_CHUNK, (i + 1) * HG_CHUNK)
        k_stack = jnp.concatenate([jnp.where(chunk_lane_head == hd, kd[sl], 0.0) for hd in range(N_HEADS)], axis=0)
        v_stack = jnp.concatenate([vh[sl] for vh in v_heads], axis=0)
        increments.append(_dot_tn(v_stack, k_stack))

    lane_head = _iota((ATT_BLOCK, width), 1) // HEAD_DIM
    mask_stack = jnp.concatenate([same_chunk_causal] * N_HEADS, axis=0)
    blocks = range(0, rows, ATT_BLOCK)
    scores = [_dot_nt(jnp.concatenate([jnp.where(lane_head == hd, qt[s:s + ATT_BLOCK], 0.0)
                                       for hd in range(N_HEADS)], axis=0), kt[s:s + ATT_BLOCK]) for s in blocks]
    work.step()

    state = st_ref[...]
    states = []
    for i in range(n_chunks):
        states.append(state)
        state = state * dec[i] + increments[i]
    st_ref[...] = state

    o_fulls = [_dot(jnp.where(mask_stack, sc, 0.0), iv[s:s + ATT_BLOCK]) for sc, s in zip(scores, blocks)]
    p_parts = []
    for i in range(n_chunks):
        sl = slice(i * HG_CHUNK, (i + 1) * HG_CHUNK)
        q_stack = jnp.concatenate([jnp.where(chunk_lane_head == hd, qd[sl], 0.0) for hd in range(N_HEADS)], axis=0)
        p_parts.append(_dot_nt(q_stack, states[i]))
    work.step()
    o_parts = []
    for o_full in o_fulls:
        o_blk = o_full[0:ATT_BLOCK]
        for hd in range(1, N_HEADS):
            o_blk = jnp.where(lane_head == hd, o_full[hd * ATT_BLOCK:(hd + 1) * ATT_BLOCK], o_blk)
        o_parts.append(o_blk)
    o_intra = jnp.concatenate(o_parts, axis=0)
    o_inter = jnp.concatenate(
        [jnp.concatenate([p[hd * HG_CHUNK:(hd + 1) * HG_CHUNK] for p in p_parts], axis=0) for hd in range(N_HEADS)],
        axis=1)
    o = o_intra + o_inter

    ms = _dot(o * o, _head_mean_matrix(width))
    o = (o * lax.rsqrt(ms + EPS)) * out_w
    return o * _sigmoid(g)


def _retention_block(q, k, v, g, gn_w, gn_b, st_ref, work):
    rows, width = q.shape
    cl = ATT_BLOCK
    lane_head_row = _iota((1, width), 1) // HEAD_DIM
    log_gamma_row = jnp.zeros((1, width), F32)
    for hd in range(N_HEADS):
        log_gamma_row = jnp.where(lane_head_row == hd, math.log1p(-2.0 ** (-5.0 - hd)), log_gamma_row)
    pos_col = _iota((cl, width), 0).astype(F32)
    rel = _iota((cl, cl), 0) - _iota((cl, cl), 1)
    rel_f = jnp.maximum(rel, 0).astype(F32)
    lane_head = _iota((cl, width), 1) // HEAD_DIM
    st_r = _iota((width, width), 0) // HEAD_DIM
    st_c = _iota((width, width), 1) // HEAD_DIM
    same_head = st_r == st_c
    q_in_decay = jnp.exp(log_gamma_row * (pos_col + 1.0))
    k_out_decay = jnp.exp(log_gamma_row * (cl - 1.0 - pos_col))
    chunk_decay = jnp.exp(log_gamma_row * float(cl))
    decay_stack = jnp.concatenate(
        [jnp.where(rel >= 0, jnp.exp(math.log1p(-2.0 ** (-5.0 - hd)) * rel_f), 0.0) for hd in range(N_HEADS)],
        axis=0)

    state = st_ref[...]
    o_parts = []
    for s in range(0, rows, cl):
        qs = q[s:s + cl] * HEAD_DIM ** -0.5
        kc, vc = k[s:s + cl], v[s:s + cl]
        o_blk = _dot_nt(qs * q_in_decay, state)
        q_stack = jnp.concatenate([jnp.where(lane_head == hd, qs, 0.0) for hd in range(N_HEADS)], axis=0)
        o_full = _dot(_dot_nt(q_stack, kc) * decay_stack, vc)
        for hd in range(N_HEADS):
            o_blk = o_blk + jnp.where(lane_head == hd, o_full[hd * cl:(hd + 1) * cl], 0.0)
        o_parts.append(o_blk)
        d_state = jnp.where(same_head, _dot_tn(vc, kc * k_out_decay), 0.0)
        state = state * chunk_decay + d_state
        if s == 0:
            work.step()
    st_ref[...] = state
    o = jnp.concatenate(o_parts, axis=0)

    mean_m = _head_mean_matrix(width)
    mu = _dot(o, mean_m)
    d = o - mu
    var = _dot(d * d, mean_m)
    work.step()
    o = d * lax.rsqrt(var + EPS) * gn_w + gn_b
    return o * (g * _sigmoid(g))


def _swa_block(q, k, v, qn_w, kn_w, sink_logit, kprev_ref, vprev_ref, block_index, work):
    rows = q.shape[0]
    kv_w = ATT_KV_HEADS * HEAD_DIM
    qn = (q * lax.rsqrt(_dot(q * q, _head_mean_matrix(q.shape[1])) + EPS)) * qn_w
    kn = (k * lax.rsqrt(_dot(k * k, _head_mean_matrix(kv_w)) + EPS)) * kn_w
    qb = ATT_BLOCK
    dist = _iota((qb, 2 * qb), 0) + qb - _iota((qb, 2 * qb), 1)
    in_window = (dist >= 0) & (dist < WINDOW)
    has_key = _iota((qb, 2 * qb), 1) + block_index * (2 * qb) >= qb
    dist_f = dist.astype(F32)
    lane_kv = _iota((qb, kv_w), 1) // HEAD_DIM
    n_blocks = rows // qb

    k_prev, v_prev = kprev_ref[...], vprev_ref[...]
    scores, values, sink_of = [], [], []
    for j in range(n_blocks):
        s = j * qb
        k_cur, v_cur = kn[s:s + qb], v[s:s + qb]
        kw = jnp.concatenate([k_prev, k_cur], axis=0)
        values.append(jnp.concatenate([v_prev, v_cur], axis=0))
        valid = in_window if j > 0 else in_window & has_key
        for kv in range(ATT_KV_HEADS):
            q_stack = jnp.concatenate(
                [jnp.where(lane_kv == kv, qn[s:s + qb, tile * kv_w:(tile + 1) * kv_w], 0.0) for tile in range(2)],
                axis=0)
            sc = _dot_nt(q_stack, kw) * HEAD_DIM ** -0.5
            for tile in range(2):
                head = kv * 2 + tile
                slope = 2.0 ** (-8.0 * (head + 1) / N_HEADS)
                scores.append(jnp.where(valid, sc[tile * qb:(tile + 1) * qb] - slope * dist_f, MASK_VALUE))
                sink_of.append(sink_logit(head))
        k_prev, v_prev = k_cur, v_cur
    work.step()
    kprev_ref[...] = k_prev
    vprev_ref[...] = v_prev

    maxes = [jnp.maximum(jnp.max(sc, axis=-1, keepdims=True), sink) for sc, sink in zip(scores, sink_of)]
    exps = [jnp.exp(sc - m) for sc, m in zip(scores, maxes)]
    work.step()
    probs = [e / (jnp.sum(e, axis=-1, keepdims=True) + jnp.exp(sink - m))
             for e, m, sink in zip(exps, maxes, sink_of)]

    o_parts = []
    for j in range(n_blocks):
        o_kv = [_dot(jnp.concatenate(probs[(j * ATT_KV_HEADS + kv) * 2:(j * ATT_KV_HEADS + kv + 1) * 2], axis=0),
                     values[j]) for kv in range(ATT_KV_HEADS)]
        o_tiles = [jnp.where(lane_kv == 0, o_kv[0][tile * qb:(tile + 1) * qb], o_kv[1][tile * qb:(tile + 1) * qb])
                   for tile in range(2)]
        o_parts.append(jnp.concatenate(o_tiles, axis=1))
        if j == 0:
            work.step()
    return jnp.concatenate(o_parts, axis=0)


def _rglru_block(xb, gb, conv_w, conv_b, wa, ba, wx, bx, lam, xpad_ref, h_ref, work):
    rows, width = xb.shape
    work.step()
    padded = jnp.concatenate([xpad_ref[...], xb], axis=0)
    xc = conv_b
    for j in range(CONV_WIDTH):
        back = CONV_WIDTH - 1 - j
        shifted = pltpu.roll(padded, back, axis=0) if back else padded
        xc = xc + shifted[SUBLANES:] * conv_w[j:j + 1, :]
    xpad_ref[...] = xb[rows - SUBLANES:rows, :]

    r_pre = _dot(xc, wa)
    i_pre = _dot(xc, wx)
    work.step()
    r = _sigmoid(r_pre + ba)
    i = _sigmoid(i_pre + bx)
    log_a = -LRU_C * r * _softplus(-lam)
    a = jnp.exp(log_a)
    u = jnp.sqrt(-jnp.tanh(log_a) * (a * a + 1.0)) * (i * xc)
    work.step()

    groups = rows // SUBLANES
    a3 = a.reshape(groups, SUBLANES, width)
    u3 = u.reshape(groups, SUBLANES, width)
    sub = _iota((groups, SUBLANES, width), 1)
    shift = 1
    while shift < SUBLANES:
        a_prev = pltpu.roll(a3, shift, axis=1)
        u_prev = pltpu.roll(u3, shift, axis=1)
        take = sub >= shift
        u3 = jnp.where(take, a3 * u_prev + u3, u3)
        a3 = jnp.where(take, a3 * a_prev, a3)
        shift *= 2
    carry = h_ref[...]
    work.step()
    h_groups = []
    for grp in range(groups):
        h_grp = u3[grp] + a3[grp] * carry
        h_groups.append(h_grp)
        carry = h_grp[SUBLANES - 1:SUBLANES, :]
    h_ref[...] = carry
    return jnp.concatenate(h_groups, axis=0) * jax.nn.gelu(gb)


def _mixer_body(layer, x_ref, lbl_ref, sinks_ref, nw_ref, win_ref, waq_ref, gbias_ref, hgw_ref, gnw_ref, gnb_ref,
                qnw_ref, knw_ref, cw_ref, cb_ref, wa_ref, ba_ref, wx_ref, bx_ref, lam_ref,
                wbr_ref, wout_ref, o_ref,
                hg_state, ret_state, kprev, vprev, xpad, lru_h, gates, ys, h_s, pj_lru, pj_swa, pj_hgrn, pj_ret):
    t = pl.program_id(1)

    @pl.when(t == 0)
    def _():
        hg_state[...] = jnp.zeros_like(hg_state)
        ret_state[...] = jnp.zeros_like(ret_state)
        kprev[...] = jnp.zeros_like(kprev)
        vprev[...] = jnp.zeros_like(vprev)
        xpad[...] = jnp.zeros_like(xpad)
        lru_h[...] = jnp.zeros_like(lru_h)

    d = x_ref.shape[1]
    w = BRANCH_WIDTH
    kv_w = ATT_KV_HEADS * HEAD_DIM

    def proj(start, size):
        if (start, size) == (8 * w, w):
            weights = waq_ref[:, 0:w]
        else:
            assert start + size <= 8 * w or start >= 9 * w
            weights = win_ref[:, start:start + size]
        return jnp.dot(h_s[...], weights, preferred_element_type=F32)

    logits = lbl_ref[...]
    e = jnp.exp(logits - jnp.max(logits, axis=0, keepdims=True))
    p = e / jnp.sum(e, axis=0, keepdims=True)
    cum = p[0:1, :]
    for j in range(1, layer + 1):
        cum = cum + p[j:j + 1, :]
    lb = cum - p[0:1, :]

    names = ("hq", "hf", "hi", "hg", "rq", "rk", "rv", "rg", "aq", "ak", "av", "lx", "lg")
    sizes = (w, w, w, w, w, w, w, w, w, kv_w, kv_w, w, w)
    offsets = {}
    off = 0
    for name, size in zip(names, sizes):
        offsets[name] = (off, size)
        off += size
    gate_off = off

    pj_of = {"lx": pj_lru, "aq": pj_swa, "hq": pj_hgrn, "rq": pj_ret}

    def proj_piece(ref, base, start, size):
        ref[:, start - base:start - base + size] = proj(start, size)

    def proj_pieces(*group):
        start, stop = offsets[group[0]][0], sum(offsets[group[-1]])
        cols = w if group[0] == "aq" else PROJ_COLS
        return [functools.partial(proj_piece, pj_of[group[0]], start, s, min(cols, stop - s))
                for s in range(start, stop, cols)]

    def projected(*group):
        base = offsets[group[0]][0]
        return [pj_of[group[0]][:, offsets[name][0] - base:sum(offsets[name]) - base] for name in group]

    def gate_chunk(c):
        cols = slice(c * GATE_COLS, (c + 1) * GATE_COLS)
        gates[:, cols] = jnp.tanh(proj(gate_off + c * GATE_COLS, GATE_COLS) + gbias_ref[:, cols])

    swa_inputs, hgrn_inputs, ret_inputs = ("aq", "ak", "av"), ("hq", "hf", "hi", "hg"), ("rq", "rk", "rv", "rg")
    work = _WorkQueue()
    for group in (swa_inputs, hgrn_inputs, ret_inputs):
        for thunk in proj_pieces(*group):
            work.add(thunk)
    for c in range(N_BRANCH * d // GATE_COLS):
        work.add(functools.partial(gate_chunk, c))

    def lru_phase(work):
        h_s[...] = _rmsnorm_rows(x_ref[...], nw_ref[...]).astype(BF16)
        for thunk in proj_pieces("lx", "lg"):
            thunk()
        ys[3] = _rglru_block(*projected("lx", "lg"), cw_ref[...], cb_ref[...], wa_ref[:, 0:w],
                             ba_ref[...], wx_ref[:, 0:w], bx_ref[...], lam_ref[...], xpad, lru_h, work).astype(BF16)

    def swa_phase(work):
        ys[2] = _swa_block(*projected("aq", "ak", "av"),
                           qnw_ref[...], knw_ref[...], lambda head: sinks_ref[layer, head], kprev, vprev, t,
                           work).astype(BF16)

    def hgrn_phase(work):
        ys[0] = _hgrn2_block(*projected("hq", "hf", "hi", "hg"), lb, hgw_ref[...], hg_state, work).astype(BF16)

    def ret_phase(work):
        ys[1] = _retention_block(*projected("rq", "rk", "rv", "rg"), gnw_ref[...], gnb_ref[...], ret_state,
                                 work).astype(BF16)

    def merge_phase(work):
        work.flush()
        merged = jnp.zeros(x_ref.shape, F32)
        for n in range(N_BRANCH):
            z = jnp.dot(ys[n], wbr_ref[n, :, 0:d], preferred_element_type=F32)
            merged = merged + (gates[:, n * d:(n + 1) * d] * z + z)
        o_ref[...] = x_ref[...] + _dot(merged, wout_ref[:, 0:d])

    lru_phase(work)
    needed = 0
    for inputs, mixer_phase in ((swa_inputs, swa_phase), (hgrn_inputs, hgrn_phase), (ret_inputs, ret_phase)):
        needed += len(proj_pieces(*inputs))
        while work.issued < needed:
            work.step()
        mixer_phase(work)
    merge_phase(work)


def _mixer(x2, n_seq, layer, lb_logits, sinks, per_layer):
    m, d = x2.shape
    seq = m // n_seq
    assert seq % MIX_ROWS == 0 and MIX_ROWS % ATT_BLOCK == 0
    steps = seq // MIX_ROWS
    w = BRANCH_WIDTH
    kv_w = ATT_KV_HEADS * HEAD_DIM
    operands = [x2, lb_logits, sinks] + list(per_layer)
    in_specs = [pl.BlockSpec((MIX_ROWS, d), lambda b, t: (b * steps + t, 0)),
                pl.BlockSpec(lb_logits.shape, lambda *_: (0, 0), pipeline_mode=pl.Buffered(1)),
                pl.BlockSpec(memory_space=pltpu.SMEM)]
    in_specs += [_layer_spec(op, layer) for op in per_layer]
    return pl.pallas_call(
        functools.partial(_mixer_body, layer),
        grid=(n_seq, steps),
        in_specs=in_specs,
        out_specs=pl.BlockSpec((MIX_ROWS, d), lambda b, t: (b * steps + t, 0)),
        out_shape=jax.ShapeDtypeStruct((m, d), F32),
        scratch_shapes=[
            pltpu.VMEM((HEAD_DIM, w), F32),
            pltpu.VMEM((w, w), F32),
            pltpu.VMEM((ATT_BLOCK, kv_w), F32),
            pltpu.VMEM((ATT_BLOCK, kv_w), F32),
            pltpu.VMEM((SUBLANES, w), F32),
            pltpu.VMEM((1, w), F32),
            pltpu.VMEM((MIX_ROWS, N_BRANCH * d), F32),
            pltpu.VMEM((N_BRANCH, MIX_ROWS, w), BF16),
            pltpu.VMEM((MIX_ROWS, d), BF16),
            pltpu.VMEM((MIX_ROWS, 2 * w), F32),
            pltpu.VMEM((MIX_ROWS, w + 2 * kv_w), F32),
            pltpu.VMEM((MIX_ROWS, 4 * w), F32),
            pltpu.VMEM((MIX_ROWS, 4 * w), F32),
        ],
        compiler_params=pltpu.CompilerParams(
            dimension_semantics=("arbitrary", "arbitrary"), vmem_limit_bytes=VMEM_LIMIT_BYTES),
        name="token_mixer",
    )(*operands)


def _block_diag(blocks):
    layers, n, c, e = blocks.shape
    eye = jnp.eye(n, dtype=blocks.dtype)
    return (eye[None, :, None, :, None] * blocks[:, :, :, None, :]).reshape(layers, n * c, n * e)


def _swa_head_order(t, axis):
    parts = jnp.split(t, N_HEADS, axis=axis)
    return jnp.concatenate([parts[0], parts[2], parts[1], parts[3]], axis=axis)


def kernel(x, ffn1_norm, ffn1_wg, ffn1_wu, ffn1_wd, mix_norm, w_in, gate_bias, hgrn_lb_logits, hgrn_out_norm, ret_gn_w, ret_gn_b, attn_q_norm, attn_k_norm, attn_sinks, lru_conv_w, lru_conv_b, lru_wa, lru_ba, lru_wx, lru_bx, lru_lambda, w_branch, w_out, ffn2_norm, ffn2_wg, ffn2_wu, ffn2_wd):
    n_seq, seq, d = x.shape
    depth = w_in.shape[0]
    w = BRANCH_WIDTH
    aq = 8 * w
    mw = _matmul_weight
    ffn1 = (ffn1_norm, mw(ffn1_wg), mw(ffn1_wu), mw(ffn1_wd))
    ffn2 = (ffn2_norm, mw(ffn2_wg), mw(ffn2_wu), mw(ffn2_wd))
    w_branch_perm = jnp.concatenate(
        [w_branch[:, :2], _swa_head_order(w_branch[:, 2], 1)[:, None], w_branch[:, 3:]], axis=1)
    gate_off = w_in.shape[2] - N_BRANCH * d
    col_scale = jnp.where(jnp.arange(w_in.shape[2]) >= gate_off, 0.5, 1.0).astype(w_in.dtype)
    mixer_operands = (
        _rows(mix_norm), mw(w_in * col_scale), mw(_swa_head_order(w_in[:, :, aq:aq + w], 2)),
        _rows(0.5 * gate_bias), _rows(hgrn_out_norm), _rows(ret_gn_w), _rows(ret_gn_b),
        _rows(jnp.tile(attn_q_norm, (1, N_HEADS))), _rows(jnp.tile(attn_k_norm, (1, ATT_KV_HEADS))),
        lru_conv_w, _rows(lru_conv_b), mw(_block_diag(lru_wa)), _rows(lru_ba),
        mw(_block_diag(lru_wx)), _rows(lru_bx), _rows(lru_lambda),
        mw(w_branch_perm), mw(0.5 * w_out))
    x2 = x.reshape(n_seq * seq, d)
    for l in range(depth):
        x2 = _ffn(x2, l, *ffn1)
        x2 = _mixer(x2, n_seq, l, hgrn_lb_logits, attn_sinks, mixer_operands)
        x2 = _ffn(x2, l, *ffn2)
    return x2.reshape(n_seq, seq, d)
```

```python
import functools
import math

import jax
import jax.numpy as jnp
from jax import lax
from jax.experimental import pallas as pl
from jax.experimental.pallas import tpu as pltpu

HEAD_DIM = 64
BRANCH_WIDTH = 256
N_BRANCH = 4
N_HEADS = 4
HG_CHUNK = 16
LB_FLOOR = 1e-30
ATT_KV_HEADS = 2
ATT_BLOCK = 128
WINDOW = 128
MASK_VALUE = -1e30
CONV_WIDTH = 4
LRU_C = 8.0
EPS = 1e-6

V7X_VMEM_BYTES = 64 * 1024 * 1024
VMEM_LIMIT_BYTES = V7X_VMEM_BYTES - 8 * 1024 * 1024
SUBLANES = 8
LANES = 128

FFN_ROWS = 1024
FFN_COLS = 256
MIX_ROWS = 512
GATE_COLS = 256
PROJ_COLS = 512

BF16 = jnp.bfloat16
F32 = jnp.float32


def _dot(a, b):
    return jnp.dot(a.astype(BF16), b.astype(BF16), preferred_element_type=F32)


def _dot_nt(a, b):
    return lax.dot_general(a.astype(BF16), b.astype(BF16), (((1,), (1,)), ((), ())),
                           preferred_element_type=F32)


def _dot_tn(a, b):
    return lax.dot_general(a.astype(BF16), b.astype(BF16), (((0,), (0,)), ((), ())),
                           preferred_element_type=F32)


def _rmsnorm_rows(x, w_row):
    return (x * lax.rsqrt(jnp.mean(x * x, axis=-1, keepdims=True) + EPS)) * w_row


def _softplus(x):
    return jnp.maximum(x, 0.0) + jnp.log1p(jnp.exp(-jnp.abs(x)))


def _sigmoid(x):
    return 0.5 * jnp.tanh(0.5 * x) + 0.5


def _iota(shape, axis):
    return lax.broadcasted_iota(jnp.int32, shape, axis)


def _matmul_weight(w):
    if (w.shape[-1] // LANES) % 4 == 0:
        w = jnp.pad(w, [(0, 0)] * (w.ndim - 1) + [(0, LANES)])
    return w.astype(BF16)


def _ffn_body(x_ref, nw_ref, wg_ref, wu_ref, wd_ref, o_ref):
    x = x_ref[...]
    d = x.shape[1]
    h = _rmsnorm_rows(x, nw_ref[...]).astype(BF16)
    d_ff = wd_ref.shape[0]
    acc = jnp.zeros(x.shape, F32)
    for c in range(d_ff // FFN_COLS):
        cols = slice(c * FFN_COLS, (c + 1) * FFN_COLS)
        g = jnp.dot(h, wg_ref[:, cols], preferred_element_type=F32)
        u = jnp.dot(h, wu_ref[:, cols], preferred_element_type=F32)
        a = (jax.nn.silu(g) * u).astype(BF16)
        acc = acc + jnp.dot(a, wd_ref[cols, 0:d], preferred_element_type=F32)
    o_ref[...] = x + 0.5 * acc


def _layer_spec(stacked, layer):
    index = (layer,) + (0,) * (stacked.ndim - 1)
    return pl.BlockSpec((None,) + stacked.shape[1:], lambda *_: index, pipeline_mode=pl.Buffered(1))


def _rows(stacked):
    return stacked.reshape(stacked.shape[0], 1, -1)


def _ffn(x2, layer, norm_w, wg, wu, wd):
    m, d = x2.shape
    d_ff = wd.shape[1]
    assert m % FFN_ROWS == 0 and d_ff % FFN_COLS == 0
    operands = (_rows(norm_w), wg, wu, wd)
    return pl.pallas_call(
        _ffn_body,
        grid=(m // FFN_ROWS,),
        in_specs=[pl.BlockSpec((FFN_ROWS, d), lambda i: (i, 0))] + [_layer_spec(op, layer) for op in operands],
        out_specs=pl.BlockSpec((FFN_ROWS, d), lambda i: (i, 0)),
        out_shape=jax.ShapeDtypeStruct((m, d), F32),
        compiler_params=pltpu.CompilerParams(
            dimension_semantics=("arbitrary",), vmem_limit_bytes=VMEM_LIMIT_BYTES),
        name="channel_mixer",
    )(x2, *operands)


class _WorkQueue:
    def __init__(self):
        self._pending = []
        self.issued = 0

    def add(self, thunk):
        self._pending.append(thunk)

    def _issue(self):
        self._pending.pop(0)()
        self.issued += 1

    def step(self, pieces=1):
        self.issue_until(min(self.issued + pieces, self.issued + len(self._pending)))

    def issue_until(self, count):
        while self.issued < count:
            self._issue()

    def flush(self):
        while self._pending:
            self._issue()


def _head_mean_matrix(width):
    r = _iota((width, width), 0) // HEAD_DIM
    c = _iota((width, width), 1) // HEAD_DIM
    return jnp.where(r == c, 1.0 / HEAD_DIM, 0.0).astype(BF16)


def _hgrn2_block(q, z, iv, g, lb, out_w, st_ref, work):
    rows, width = q.shape
    n_chunks = rows // HG_CHUNK
    log_lb = jnp.log(jnp.maximum(lb, LB_FLOOR))
    r_i = _iota((ATT_BLOCK, ATT_BLOCK), 0)
    c_i = _iota((ATT_BLOCK, ATT_BLOCK), 1)
    same_chunk_causal = (r_i // HG_CHUNK == c_i // HG_CHUNK) & (c_i <= r_i)
    tri = jnp.where(same_chunk_causal, 1.0, 0.0).astype(BF16)

    block_chunks = ATT_BLOCK // HG_CHUNK
    qt, kt, qd, kd, dec = [], [], [], [], []
    for s in range(0, rows, ATT_BLOCK):
        work.step()
        zb = z[s:s + ATT_BLOCK]
        c = jnp.log1p(-lb) - _softplus(-zb)
        log_f = jnp.maximum(log_lb, c) + jnp.log1p(jnp.exp(-jnp.abs(log_lb - c)))
        lf_hi = log_f.astype(BF16)
        lf_lo = (log_f - lf_hi.astype(F32)).astype(BF16)
        b = (jnp.dot(tri, lf_hi, preferred_element_type=F32) + jnp.dot(tri, lf_lo, preferred_element_type=F32))
        b3 = b.reshape(block_chunks, HG_CHUNK, width)
        b_ref = b3[:, HG_CHUNK // 2:HG_CHUNK // 2 + 1, :]
        b_last = b3[:, HG_CHUNK - 1:HG_CHUNK, :]
        q3 = q[s:s + ATT_BLOCK].reshape(block_chunks, HG_CHUNK, width)
        k3 = ((1.0 - lb) * _sigmoid(-zb)).reshape(block_chunks, HG_CHUNK, width)
        qt.append((q3 * jnp.exp(b3 - b_ref)).reshape(ATT_BLOCK, width))
        kt.append((k3 * jnp.exp(b_ref - b3)).reshape(ATT_BLOCK, width))
        qd.append((q3 * jnp.exp(b3)).reshape(ATT_BLOCK, width))
        kd.append((k3 * jnp.exp(b_last - b3)).reshape(ATT_BLOCK, width))
        dec.append(jnp.exp(b_last))
    qt, kt, qd, kd, dec = (jnp.concatenate(parts, axis=0) for parts in (qt, kt, qd, kd, dec))

    chunk_lane_head = _iota((HG_CHUNK, width), 1) // HEAD_DIM
    v_heads = [iv[:, hd * HEAD_DIM:(hd + 1) * HEAD_DIM] for hd in range(N_HEADS)]
    increments = []
    for i in range(n_chunks):
        sl = slice(i * HG_CHUNK, (i + 1) * HG_CHUNK)
        k_stack = jnp.concatenate([jnp.where(chunk_lane_head == hd, kd[sl], 0.0) for hd in range(N_HEADS)], axis=0)
        v_stack = jnp.concatenate([vh[sl] for vh in v_heads], axis=0)
        increments.append(_dot_tn(v_stack, k_stack))

    lane_head = _iota((ATT_BLOCK, width), 1) // HEAD_DIM
    mask_stack = jnp.concatenate([same_chunk_causal] * N_HEADS, axis=0)
    blocks = range(0, rows, ATT_BLOCK)
    scores = [_dot_nt(jnp.concatenate([jnp.where(lane_head == hd, qt[s:s + ATT_BLOCK], 0.0)
                                       for hd in range(N_HEADS)], axis=0), kt[s:s + ATT_BLOCK]) for s in blocks]
    work.step()

    state = st_ref[...]
    states = []
    for i in range(n_chunks):
        states.append(state)
        state = state * dec[i] + increments[i]
    st_ref[...] = state

    o_fulls = [_dot(jnp.where(mask_stack, sc, 0.0), iv[s:s + ATT_BLOCK]) for sc, s in zip(scores, blocks)]
    p_parts = []
    for i in range(n_chunks):
        sl = slice(i * HG_CHUNK, (i + 1) * HG_CHUNK)
        q_stack = jnp.concatenate([jnp.where(chunk_lane_head == hd, qd[sl], 0.0) for hd in range(N_HEADS)], axis=0)
        p_parts.append(_dot_nt(q_stack, states[i]))
    work.step()
    o_parts = []
    for o_full in o_fulls:
        o_blk = o_full[0:ATT_BLOCK]
        for hd in range(1, N_HEADS):
            o_blk = jnp.where(lane_head == hd, o_full[hd * ATT_BLOCK:(hd + 1) * ATT_BLOCK], o_blk)
        o_parts.append(o_blk)
    o_intra = jnp.concatenate(o_parts, axis=0)
    o_inter = jnp.concatenate(
        [jnp.concatenate([p[hd * HG_CHUNK:(hd + 1) * HG_CHUNK] for p in p_parts], axis=0) for hd in range(N_HEADS)],
        axis=1)
    o = o_intra + o_inter

    ms = _dot(o * o, _head_mean_matrix(width))
    o = (o * lax.rsqrt(ms + EPS)) * out_w
    return o * _sigmoid(g)


def _retention_block(q, k, v, g, gn_w, gn_b, st_ref, work):
    rows, width = q.shape
    cl = ATT_BLOCK
    lane_head_row = _iota((1, width), 1) // HEAD_DIM
    log_gamma_row = jnp.zeros((1, width), F32)
    for hd in range(N_HEADS):
        log_gamma_row = jnp.where(lane_head_row == hd, math.log1p(-2.0 ** (-5.0 - hd)), log_gamma_row)
    pos_col = _iota((cl, width), 0).astype(F32)
    rel = _iota((cl, cl), 0) - _iota((cl, cl), 1)
    rel_f = jnp.maximum(rel, 0).astype(F32)
    lane_head = _iota((cl, width), 1) // HEAD_DIM
    st_r = _iota((width, width), 0) // HEAD_DIM
    st_c = _iota((width, width), 1) // HEAD_DIM
    same_head = st_r == st_c
    q_in_decay = jnp.exp(log_gamma_row * (pos_col + 1.0))
    k_out_decay = jnp.exp(log_gamma_row * (cl - 1.0 - pos_col))
    chunk_decay = jnp.exp(log_gamma_row * float(cl))
    decay_stack = jnp.concatenate(
        [jnp.where(rel >= 0, jnp.exp(math.log1p(-2.0 ** (-5.0 - hd)) * rel_f), 0.0) for hd in range(N_HEADS)],
        axis=0)

    state = st_ref[...]
    o_parts = []
    for s in range(0, rows, cl):
        qs = q[s:s + cl] * HEAD_DIM ** -0.5
        kc, vc = k[s:s + cl], v[s:s + cl]
        o_blk = _dot_nt(qs * q_in_decay, state)
        q_stack = jnp.concatenate([jnp.where(lane_head == hd, qs, 0.0) for hd in range(N_HEADS)], axis=0)
        o_full = _dot(_dot_nt(q_stack, kc) * decay_stack, vc)
        for hd in range(N_HEADS):
            o_blk = o_blk + jnp.where(lane_head == hd, o_full[hd * cl:(hd + 1) * cl], 0.0)
        o_parts.append(o_blk)
        d_state = jnp.where(same_head, _dot_tn(vc, kc * k_out_decay), 0.0)
        state = state * chunk_decay + d_state
        if s == 0:
            work.step()
    st_ref[...] = state
    o = jnp.concatenate(o_parts, axis=0)

    mean_m = _head_mean_matrix(width)
    mu = _dot(o, mean_m)
    d = o - mu
    var = _dot(d * d, mean_m)
    work.step()
    o = d * lax.rsqrt(var + EPS) * gn_w + gn_b
    return o * (g * _sigmoid(g))


def _swa_block(q, k, v, qn_w, kn_w, sink_logit, kprev_ref, vprev_ref, block_index, work):
    rows = q.shape[0]
    kv_w = ATT_KV_HEADS * HEAD_DIM
    qn = (q * lax.rsqrt(_dot(q * q, _head_mean_matrix(q.shape[1])) + EPS)) * qn_w
    kn = (k * lax.rsqrt(_dot(k * k, _head_mean_matrix(kv_w)) + EPS)) * kn_w
    qb = ATT_BLOCK
    dist = _iota((qb, 2 * qb), 0) + qb - _iota((qb, 2 * qb), 1)
    in_window = (dist >= 0) & (dist < WINDOW)
    has_key = _iota((qb, 2 * qb), 1) + block_index * (2 * qb) >= qb
    dist_f = dist.astype(F32)
    lane_kv = _iota((qb, kv_w), 1) // HEAD_DIM
    n_blocks = rows // qb

    k_prev, v_prev = kprev_ref[...], vprev_ref[...]
    scores, values, sink_of = [], [], []
    for j in range(n_blocks):
        s = j * qb
        k_cur, v_cur = kn[s:s + qb], v[s:s + qb]
        kw = jnp.concatenate([k_prev, k_cur], axis=0)
        values.append(jnp.concatenate([v_prev, v_cur], axis=0))
        valid = in_window if j > 0 else in_window & has_key
        for kv in range(ATT_KV_HEADS):
            q_stack = jnp.concatenate(
                [jnp.where(lane_kv == kv, qn[s:s + qb, tile * kv_w:(tile + 1) * kv_w], 0.0) for tile in range(2)],
                axis=0)
            sc = _dot_nt(q_stack, kw) * HEAD_DIM ** -0.5
            for tile in range(2):
                head = kv * 2 + tile
                slope = 2.0 ** (-8.0 * (head + 1) / N_HEADS)
                scores.append(jnp.where(valid, sc[tile * qb:(tile + 1) * qb] - slope * dist_f, MASK_VALUE))
                sink_of.append(sink_logit(head))
        k_prev, v_prev = k_cur, v_cur
    work.step()
    kprev_ref[...] = k_prev
    vprev_ref[...] = v_prev

    maxes = [jnp.maximum(jnp.max(sc, axis=-1, keepdims=True), sink) for sc, sink in zip(scores, sink_of)]
    exps = [jnp.exp(sc - m) for sc, m in zip(scores, maxes)]
    work.step()
    probs = [e / (jnp.sum(e, axis=-1, keepdims=True) + jnp.exp(sink - m))
             for e, m, sink in zip(exps, maxes, sink_of)]

    o_parts = []
    for j in range(n_blocks):
        o_kv = [_dot(jnp.concatenate(probs[(j * ATT_KV_HEADS + kv) * 2:(j * ATT_KV_HEADS + kv + 1) * 2], axis=0),
                     values[j]) for kv in range(ATT_KV_HEADS)]
        o_tiles = [jnp.where(lane_kv == 0, o_kv[0][tile * qb:(tile + 1) * qb], o_kv[1][tile * qb:(tile + 1) * qb])
                   for tile in range(2)]
        o_parts.append(jnp.concatenate(o_tiles, axis=1))
        if j == 0:
            work.step()
    return jnp.concatenate(o_parts, axis=0)


def _rglru_block(xb, gb, conv_w, conv_b, wa, ba, wx, bx, lam, xpad_ref, h_ref, work):
    rows, width = xb.shape
    work.step()
    padded = jnp.concatenate([xpad_ref[...], xb], axis=0)
    xc = conv_b
    for j in range(CONV_WIDTH):
        back = CONV_WIDTH - 1 - j
        shifted = pltpu.roll(padded, back, axis=0) if back else padded
        xc = xc + shifted[SUBLANES:] * conv_w[j:j + 1, :]
    xpad_ref[...] = xb[rows - SUBLANES:rows, :]

    r_pre = _dot(xc, wa)
    i_pre = _dot(xc, wx)
    block_groups = ATT_BLOCK // SUBLANES
    sub = _iota((block_groups, SUBLANES, width), 1)
    scanned = []
    for s in range(0, rows, ATT_BLOCK):
        work.step()
        blk = slice(s, s + ATT_BLOCK)
        r = _sigmoid(r_pre[blk] + ba)
        i = _sigmoid(i_pre[blk] + bx)
        log_a = -LRU_C * r * _softplus(-lam)
        a = jnp.exp(log_a)
        u = jnp.sqrt(-jnp.tanh(log_a) * (a * a + 1.0)) * (i * xc[blk])
        a3 = a.reshape(block_groups, SUBLANES, width)
        u3 = u.reshape(block_groups, SUBLANES, width)
        shift = 1
        while shift < SUBLANES:
            a_prev = pltpu.roll(a3, shift, axis=1)
            u_prev = pltpu.roll(u3, shift, axis=1)
            take = sub >= shift
            u3 = jnp.where(take, a3 * u_prev + u3, u3)
            a3 = jnp.where(take, a3 * a_prev, a3)
            shift *= 2
        scanned.append((a3, u3))
    carry = h_ref[...]
    work.step(2)
    h_groups = []
    for a3, u3 in scanned:
        for grp in range(block_groups):
            h_grp = u3[grp] + a3[grp] * carry
            h_groups.append(h_grp)
            carry = h_grp[SUBLANES - 1:SUBLANES, :]
    h_ref[...] = carry
    return jnp.concatenate(h_groups, axis=0) * jax.nn.gelu(gb)


def _mixer_body(layer, x_ref, lbl_ref, sinks_ref, nw_ref, win_ref, waq_ref, gbias_ref, hgw_ref, gnw_ref, gnb_ref,
                qnw_ref, knw_ref, cw_ref, cb_ref, wa_ref, ba_ref, wx_ref, bx_ref, lam_ref,
                wbr_ref, wout_ref, o_ref,
                hg_state, ret_state, kprev, vprev, xpad, lru_h, gates, ys, h_s, pj_lru, pj_swa, pj_hgrn, pj_ret):
    t = pl.program_id(1)

    @pl.when(t == 0)
    def _():
        hg_state[...] = jnp.zeros_like(hg_state)
        ret_state[...] = jnp.zeros_like(ret_state)
        kprev[...] = jnp.zeros_like(kprev)
        vprev[...] = jnp.zeros_like(vprev)
        xpad[...] = jnp.zeros_like(xpad)
        lru_h[...] = jnp.zeros_like(lru_h)

    d = x_ref.shape[1]
    w = BRANCH_WIDTH
    kv_w = ATT_KV_HEADS * HEAD_DIM

    def proj(start, size):
        if (start, size) == (8 * w, w):
            weights = waq_ref[:, 0:w]
        else:
            assert start + size <= 8 * w or start >= 9 * w
            weights = win_ref[:, start:start + size]
        return jnp.dot(h_s[...], weights, preferred_element_type=F32)

    logits = lbl_ref[...]
    e = jnp.exp(logits - jnp.max(logits, axis=0, keepdims=True))
    p = e / jnp.sum(e, axis=0, keepdims=True)
    cum = p[0:1, :]
    for j in range(1, layer + 1):
        cum = cum + p[j:j + 1, :]
    lb = cum - p[0:1, :]

    names = ("hq", "hf", "hi", "hg", "rq", "rk", "rv", "rg", "aq", "ak", "av", "lx", "lg")
    sizes = (w, w, w, w, w, w, w, w, w, kv_w, kv_w, w, w)
    offsets = {}
    off = 0
    for name, size in zip(names, sizes):
        offsets[name] = (off, size)
        off += size
    gate_off = off

    pj_of = {"lx": pj_lru, "aq": pj_swa, "hq": pj_hgrn, "rq": pj_ret}

    def proj_piece(ref, base, start, size):
        ref[:, start - base:start - base + size] = proj(start, size)

    def proj_pieces(*group):
        start, stop = offsets[group[0]][0], sum(offsets[group[-1]])
        cols = w if group[0] == "aq" else PROJ_COLS
        return [functools.partial(proj_piece, pj_of[group[0]], start, s, min(cols, stop - s))
                for s in range(start, stop, cols)]

    def projected(*group):
        base = offsets[group[0]][0]
        return [pj_of[group[0]][:, offsets[name][0] - base:sum(offsets[name]) - base] for name in group]

    def gate_chunk(c):
        cols = slice(c * GATE_COLS, (c + 1) * GATE_COLS)
        gates[:, cols] = jnp.tanh(proj(gate_off + c * GATE_COLS, GATE_COLS) + gbias_ref[:, cols])

    swa_inputs, hgrn_inputs, ret_inputs = ("aq", "ak", "av"), ("hq", "hf", "hi", "hg"), ("rq", "rk", "rv", "rg")
    work = _WorkQueue()
    for group in (swa_inputs, hgrn_inputs, ret_inputs):
        for thunk in proj_pieces(*group):
            work.add(thunk)
    for c in range(N_BRANCH * d // GATE_COLS):
        work.add(functools.partial(gate_chunk, c))

    def lru_phase(work):
        h_s[...] = _rmsnorm_rows(x_ref[...], nw_ref[...]).astype(BF16)
        for thunk in proj_pieces("lx", "lg"):
            thunk()
        ys[3] = _rglru_block(*projected("lx", "lg"), cw_ref[...], cb_ref[...], wa_ref[:, 0:w],
                             ba_ref[...], wx_ref[:, 0:w], bx_ref[...], lam_ref[...], xpad, lru_h, work).astype(BF16)

    def swa_phase(work):
        ys[2] = _swa_block(*projected("aq", "ak", "av"),
                           qnw_ref[...], knw_ref[...], lambda head: sinks_ref[layer, head], kprev, vprev, t,
                           work).astype(BF16)

    def hgrn_phase(work):
        ys[0] = _hgrn2_block(*projected("hq", "hf", "hi", "hg"), lb, hgw_ref[...], hg_state, work).astype(BF16)

    def ret_phase(work):
        ys[1] = _retention_block(*projected("rq", "rk", "rv", "rg"), gnw_ref[...], gnb_ref[...], ret_state,
                                 work).astype(BF16)

    def merge_phase(work):
        work.flush()
        merged = jnp.zeros(x_ref.shape, F32)
        for n in range(N_BRANCH):
            z = jnp.dot(ys[n], wbr_ref[n, :, 0:d], preferred_element_type=F32)
            merged = merged + (gates[:, n * d:(n + 1) * d] * z + z)
        o_ref[...] = x_ref[...] + _dot(merged, wout_ref[:, 0:d])

    lru_phase(work)
    needed = 0
    for inputs, mixer_phase in ((swa_inputs, swa_phase), (hgrn_inputs, hgrn_phase), (ret_inputs, ret_phase)):
        needed += len(proj_pieces(*inputs))
        work.issue_until(needed)
        mixer_phase(work)
    merge_phase(work)


def _mixer(x2, n_seq, layer, lb_logits, sinks, per_layer):
    m, d = x2.shape
    seq = m // n_seq
    assert seq % MIX_ROWS == 0 and MIX_ROWS % ATT_BLOCK == 0
    steps = seq // MIX_ROWS
    w = BRANCH_WIDTH
    kv_w = ATT_KV_HEADS * HEAD_DIM
    operands = [x2, lb_logits, sinks] + list(per_layer)
    in_specs = [pl.BlockSpec((MIX_ROWS, d), lambda b, t: (b * steps + t, 0)),
                pl.BlockSpec(lb_logits.shape, lambda *_: (0, 0), pipeline_mode=pl.Buffered(1)),
                pl.BlockSpec(memory_space=pltpu.SMEM)]
    in_specs += [_layer_spec(op, layer) for op in per_layer]
    return pl.pallas_call(
        functools.partial(_mixer_body, layer),
        grid=(n_seq, steps),
        in_specs=in_specs,
        out_specs=pl.BlockSpec((MIX_ROWS, d), lambda b, t: (b * steps + t, 0)),
        out_shape=jax.ShapeDtypeStruct((m, d), F32),
        scratch_shapes=[
            pltpu.VMEM((HEAD_DIM, w), F32),
            pltpu.VMEM((w, w), F32),
            pltpu.VMEM((ATT_BLOCK, kv_w), F32),
            pltpu.VMEM((ATT_BLOCK, kv_w), F32),
            pltpu.VMEM((SUBLANES, w), F32),
            pltpu.VMEM((1, w), F32),
            pltpu.VMEM((MIX_ROWS, N_BRANCH * d), F32),
            pltpu.VMEM((N_BRANCH, MIX_ROWS, w), BF16),
            pltpu.VMEM((MIX_ROWS, d), BF16),
            pltpu.VMEM((MIX_ROWS, 2 * w), F32),
            pltpu.VMEM((MIX_ROWS, w + 2 * kv_w), F32),
            pltpu.VMEM((MIX_ROWS, 4 * w), F32),
            pltpu.VMEM((MIX_ROWS, 4 * w), F32),
        ],
        compiler_params=pltpu.CompilerParams(
            dimension_semantics=("arbitrary", "arbitrary"), vmem_limit_bytes=VMEM_LIMIT_BYTES),
        name="token_mixer",
    )(*operands)


def _block_diag(blocks):
    layers, n, c, e = blocks.shape
    eye = jnp.eye(n, dtype=blocks.dtype)
    return (eye[None, :, None, :, None] * blocks[:, :, :, None, :]).reshape(layers, n * c, n * e)


def _swa_head_order(t, axis):
    parts = jnp.split(t, N_HEADS, axis=axis)
    return jnp.concatenate([parts[0], parts[2], parts[1], parts[3]], axis=axis)


def kernel(x, ffn1_norm, ffn1_wg, ffn1_wu, ffn1_wd, mix_norm, w_in, gate_bias, hgrn_lb_logits, hgrn_out_norm, ret_gn_w, ret_gn_b, attn_q_norm, attn_k_norm, attn_sinks, lru_conv_w, lru_conv_b, lru_wa, lru_ba, lru_wx, lru_bx, lru_lambda, w_branch, w_out, ffn2_norm, ffn2_wg, ffn2_wu, ffn2_wd):
    n_seq, seq, d = x.shape
    depth = w_in.shape[0]
    w = BRANCH_WIDTH
    aq = 8 * w
    mw = _matmul_weight
    ffn1 = (ffn1_norm, mw(ffn1_wg), mw(ffn1_wu), mw(ffn1_wd))
    ffn2 = (ffn2_norm, mw(ffn2_wg), mw(ffn2_wu), mw(ffn2_wd))
    w_branch_perm = jnp.concatenate(
        [w_branch[:, :2], _swa_head_order(w_branch[:, 2], 1)[:, None], w_branch[:, 3:]], axis=1)
    gate_off = w_in.shape[2] - N_BRANCH * d
    col_scale = jnp.where(jnp.arange(w_in.shape[2]) >= gate_off, 0.5, 1.0).astype(w_in.dtype)
    mixer_operands = (
        _rows(mix_norm), mw(w_in * col_scale), mw(_swa_head_order(w_in[:, :, aq:aq + w], 2)),
        _rows(0.5 * gate_bias), _rows(hgrn_out_norm), _rows(ret_gn_w), _rows(ret_gn_b),
        _rows(jnp.tile(attn_q_norm, (1, N_HEADS))), _rows(jnp.tile(attn_k_norm, (1, ATT_KV_HEADS))),
        lru_conv_w, _rows(lru_conv_b), mw(_block_diag(lru_wa)), _rows(lru_ba),
        mw(_block_diag(lru_wx)), _rows(lru_bx), _rows(lru_lambda),
        mw(w_branch_perm), mw(0.5 * w_out))
    x2 = x.reshape(n_seq * seq, d)
    for l in range(depth):
        x2 = _ffn(x2, l, *ffn1)
        x2 = _mixer(x2, n_seq, l, hgrn_lb_logits, attn_sinks, mixer_operands)
        x2 = _ffn(x2, l, *ffn2)
    return x2.reshape(n_seq, seq, d)
```

```python
import functools
import math

import jax
import jax.numpy as jnp
from jax import lax
from jax.experimental import pallas as pl
from jax.experimental.pallas import tpu as pltpu

HEAD_DIM = 64
BRANCH_WIDTH = 256
N_BRANCH = 4
N_HEADS = 4
HG_CHUNK = 16
LB_FLOOR = 1e-30
ATT_KV_HEADS = 2
ATT_BLOCK = 128
WINDOW = 128
MASK_VALUE = -1e30
CONV_WIDTH = 4
LRU_C = 8.0
EPS = 1e-6

V7X_VMEM_BYTES = 64 * 1024 * 1024
VMEM_LIMIT_BYTES = V7X_VMEM_BYTES - 8 * 1024 * 1024
SUBLANES = 8
LANES = 128

FFN_ROWS = 1024
FFN_COLS = 256
MIX_ROWS = 512
GATE_COLS = 256
PROJ_COLS = 512

BF16 = jnp.bfloat16
F32 = jnp.float32


def _dot(a, b):
    return jnp.dot(a.astype(BF16), b.astype(BF16), preferred_element_type=F32)


def _dot_nt(a, b):
    return lax.dot_general(a.astype(BF16), b.astype(BF16), (((1,), (1,)), ((), ())),
                           preferred_element_type=F32)


def _dot_tn(a, b):
    return lax.dot_general(a.astype(BF16), b.astype(BF16), (((0,), (0,)), ((), ())),
                           preferred_element_type=F32)


def _rmsnorm_rows(x, w_row):
    return (x * lax.rsqrt(jnp.mean(x * x, axis=-1, keepdims=True) + EPS)) * w_row


def _softplus(x):
    return jnp.maximum(x, 0.0) + jnp.log1p(jnp.exp(-jnp.abs(x)))


def _sigmoid(x):
    return 0.5 * jnp.tanh(0.5 * x) + 0.5


def _iota(shape, axis):
    return lax.broadcasted_iota(jnp.int32, shape, axis)


def _matmul_weight(w):
    if (w.shape[-1] // LANES) % 4 == 0:
        w = jnp.pad(w, [(0, 0)] * (w.ndim - 1) + [(0, LANES)])
    return w.astype(BF16)


def _ffn_body(x_ref, nw_ref, wg_ref, wu_ref, wd_ref, o_ref):
    x = x_ref[...]
    d = x.shape[1]
    h = _rmsnorm_rows(x, nw_ref[...]).astype(BF16)
    d_ff = wd_ref.shape[0]
    acts = []
    for c in range(d_ff // FFN_COLS):
        cols = slice(c * FFN_COLS, (c + 1) * FFN_COLS)
        g = jnp.dot(h, wg_ref[:, cols], preferred_element_type=F32)
        u = jnp.dot(h, wu_ref[:, cols], preferred_element_type=F32)
        acts.append((jax.nn.silu(g) * u).astype(BF16))
    act = jnp.concatenate(acts, axis=1)
    o_ref[...] = x + 0.5 * jnp.dot(act, wd_ref[:, 0:d], preferred_element_type=F32)


def _layer_spec(stacked, layer):
    index = (layer,) + (0,) * (stacked.ndim - 1)
    return pl.BlockSpec((None,) + stacked.shape[1:], lambda *_: index, pipeline_mode=pl.Buffered(1))


def _rows(stacked):
    return stacked.reshape(stacked.shape[0], 1, -1)


def _ffn(x2, layer, norm_w, wg, wu, wd):
    m, d = x2.shape
    d_ff = wd.shape[1]
    assert m % FFN_ROWS == 0 and d_ff % FFN_COLS == 0
    operands = (_rows(norm_w), wg, wu, wd)
    return pl.pallas_call(
        _ffn_body,
        grid=(m // FFN_ROWS,),
        in_specs=[pl.BlockSpec((FFN_ROWS, d), lambda i: (i, 0))] + [_layer_spec(op, layer) for op in operands],
        out_specs=pl.BlockSpec((FFN_ROWS, d), lambda i: (i, 0)),
        out_shape=jax.ShapeDtypeStruct((m, d), F32),
        compiler_params=pltpu.CompilerParams(
            dimension_semantics=("arbitrary",), vmem_limit_bytes=VMEM_LIMIT_BYTES),
        name="channel_mixer",
    )(x2, *operands)


class _WorkQueue:
    def __init__(self):
        self._pending = []
        self.issued = 0

    def add(self, thunk):
        self._pending.append(thunk)

    def _issue(self):
        self._pending.pop(0)()
        self.issued += 1

    def step(self, pieces=1):
        self.issue_until(min(self.issued + pieces, self.issued + len(self._pending)))

    def issue_until(self, count):
        while self.issued < count:
            self._issue()

    def flush(self):
        while self._pending:
            self._issue()


def _head_mean_matrix(width):
    r = _iota((width, width), 0) // HEAD_DIM
    c = _iota((width, width), 1) // HEAD_DIM
    return jnp.where(r == c, 1.0 / HEAD_DIM, 0.0).astype(BF16)


def _hgrn2_block(q, z, iv, g, lb, out_w, st_ref, work):
    rows, width = q.shape
    n_chunks = rows // HG_CHUNK
    log_lb = jnp.log(jnp.maximum(lb, LB_FLOOR))
    r_i = _iota((ATT_BLOCK, ATT_BLOCK), 0)
    c_i = _iota((ATT_BLOCK, ATT_BLOCK), 1)
    same_chunk_causal = (r_i // HG_CHUNK == c_i // HG_CHUNK) & (c_i <= r_i)
    tri = jnp.where(same_chunk_causal, 1.0, 0.0).astype(BF16)

    block_chunks = ATT_BLOCK // HG_CHUNK
    qt, kt, qd, kd, dec = [], [], [], [], []
    for s in range(0, rows, ATT_BLOCK):
        work.step()
        zb = z[s:s + ATT_BLOCK]
        c = jnp.log1p(-lb) - _softplus(-zb)
        log_f = jnp.maximum(log_lb, c) + jnp.log1p(jnp.exp(-jnp.abs(log_lb - c)))
        lf_hi = log_f.astype(BF16)
        lf_lo = (log_f - lf_hi.astype(F32)).astype(BF16)
        b = (jnp.dot(tri, lf_hi, preferred_element_type=F32) + jnp.dot(tri, lf_lo, preferred_element_type=F32))
        b3 = b.reshape(block_chunks, HG_CHUNK, width)
        b_ref = b3[:, HG_CHUNK // 2:HG_CHUNK // 2 + 1, :]
        b_last = b3[:, HG_CHUNK - 1:HG_CHUNK, :]
        q3 = q[s:s + ATT_BLOCK].reshape(block_chunks, HG_CHUNK, width)
        k3 = ((1.0 - lb) * _sigmoid(-zb)).reshape(block_chunks, HG_CHUNK, width)
        qt.append((q3 * jnp.exp(b3 - b_ref)).reshape(ATT_BLOCK, width))
        kt.append((k3 * jnp.exp(b_ref - b3)).reshape(ATT_BLOCK, width))
        qd.append((q3 * jnp.exp(b3)).reshape(ATT_BLOCK, width))
        kd.append((k3 * jnp.exp(b_last - b3)).reshape(ATT_BLOCK, width))
        dec.append(jnp.exp(b_last))
    qt, kt, qd, kd, dec = (jnp.concatenate(parts, axis=0) for parts in (qt, kt, qd, kd, dec))

    chunk_lane_head = _iota((HG_CHUNK, width), 1) // HEAD_DIM
    v_heads = [iv[:, hd * HEAD_DIM:(hd + 1) * HEAD_DIM] for hd in range(N_HEADS)]
    increments = []
    for i in range(n_chunks):
        sl = slice(i * HG_CHUNK, (i + 1) * HG_CHUNK)
        k_stack = jnp.concatenate([jnp.where(chunk_lane_head == hd, kd[sl], 0.0) for hd in range(N_HEADS)], axis=0)
        v_stack = jnp.concatenate([vh[sl] for vh in v_heads], axis=0)
        increments.append(_dot_tn(v_stack, k_stack))

    lane_head = _iota((ATT_BLOCK, width), 1) // HEAD_DIM
    mask_stack = jnp.concatenate([same_chunk_causal] * N_HEADS, axis=0)
    blocks = range(0, rows, ATT_BLOCK)
    scores = [_dot_nt(jnp.concatenate([jnp.where(lane_head == hd, qt[s:s + ATT_BLOCK], 0.0)
                                       for hd in range(N_HEADS)], axis=0), kt[s:s + ATT_BLOCK]) for s in blocks]
    work.step()

    state = st_ref[...]
    states = []
    for i in range(n_chunks):
        states.append(state)
        state = state * dec[i] + increments[i]
    st_ref[...] = state

    o_fulls = [_dot(jnp.where(mask_stack, sc, 0.0), iv[s:s + ATT_BLOCK]) for sc, s in zip(scores, blocks)]
    p_parts = []
    for i in range(n_chunks):
        sl = slice(i * HG_CHUNK, (i + 1) * HG_CHUNK)
        q_stack = jnp.concatenate([jnp.where(chunk_lane_head == hd, qd[sl], 0.0) for hd in range(N_HEADS)], axis=0)
        p_parts.append(_dot_nt(q_stack, states[i]))
    work.step()
    o_parts = []
    for o_full in o_fulls:
        o_blk = o_full[0:ATT_BLOCK]
        for hd in range(1, N_HEADS):
            o_blk = jnp.where(lane_head == hd, o_full[hd * ATT_BLOCK:(hd + 1) * ATT_BLOCK], o_blk)
        o_parts.append(o_blk)
    o_intra = jnp.concatenate(o_parts, axis=0)
    o_inter = jnp.concatenate(
        [jnp.concatenate([p[hd * HG_CHUNK:(hd + 1) * HG_CHUNK] for p in p_parts], axis=0) for hd in range(N_HEADS)],
        axis=1)
    o = o_intra + o_inter

    ms = _dot(o * o, _head_mean_matrix(width))
    o = (o * lax.rsqrt(ms + EPS)) * out_w
    return o * _sigmoid(g)


def _retention_block(q, k, v, g, gn_w, gn_b, st_ref, work):
    rows, width = q.shape
    cl = ATT_BLOCK
    lane_head_row = _iota((1, width), 1) // HEAD_DIM
    log_gamma_row = jnp.zeros((1, width), F32)
    for hd in range(N_HEADS):
        log_gamma_row = jnp.where(lane_head_row == hd, math.log1p(-2.0 ** (-5.0 - hd)), log_gamma_row)
    pos_col = _iota((cl, width), 0).astype(F32)
    rel = _iota((cl, cl), 0) - _iota((cl, cl), 1)
    rel_f = jnp.maximum(rel, 0).astype(F32)
    lane_head = _iota((cl, width), 1) // HEAD_DIM
    st_r = _iota((width, width), 0) // HEAD_DIM
    st_c = _iota((width, width), 1) // HEAD_DIM
    same_head = st_r == st_c
    q_in_decay = jnp.exp(log_gamma_row * (pos_col + 1.0))
    k_out_decay = jnp.exp(log_gamma_row * (cl - 1.0 - pos_col))
    chunk_decay = jnp.exp(log_gamma_row * float(cl))
    decay_stack = jnp.concatenate(
        [jnp.where(rel >= 0, jnp.exp(math.log1p(-2.0 ** (-5.0 - hd)) * rel_f), 0.0) for hd in range(N_HEADS)],
        axis=0)

    state = st_ref[...]
    o_parts = []
    for s in range(0, rows, cl):
        qs = q[s:s + cl] * HEAD_DIM ** -0.5
        kc, vc = k[s:s + cl], v[s:s + cl]
        o_blk = _dot_nt(qs * q_in_decay, state)
        q_stack = jnp.concatenate([jnp.where(lane_head == hd, qs, 0.0) for hd in range(N_HEADS)], axis=0)
        o_full = _dot(_dot_nt(q_stack, kc) * decay_stack, vc)
        for hd in range(N_HEADS):
            o_blk = o_blk + jnp.where(lane_head == hd, o_full[hd * cl:(hd + 1) * cl], 0.0)
        o_parts.append(o_blk)
        d_state = jnp.where(same_head, _dot_tn(vc, kc * k_out_decay), 0.0)
        state = state * chunk_decay + d_state
        if s == 0:
            work.step()
    st_ref[...] = state
    o = jnp.concatenate(o_parts, axis=0)

    mean_m = _head_mean_matrix(width)
    mu = _dot(o, mean_m)
    d = o - mu
    var = _dot(d * d, mean_m)
    work.step()
    o = d * lax.rsqrt(var + EPS) * gn_w + gn_b
    return o * (g * _sigmoid(g))


def _swa_block(q, k, v, qn_w, kn_w, sink_logit, kprev_ref, vprev_ref, block_index, work):
    rows = q.shape[0]
    kv_w = ATT_KV_HEADS * HEAD_DIM
    qn = (q * lax.rsqrt(_dot(q * q, _head_mean_matrix(q.shape[1])) + EPS)) * qn_w
    kn = (k * lax.rsqrt(_dot(k * k, _head_mean_matrix(kv_w)) + EPS)) * kn_w
    qb = ATT_BLOCK
    dist = _iota((qb, 2 * qb), 0) + qb - _iota((qb, 2 * qb), 1)
    in_window = (dist >= 0) & (dist < WINDOW)
    has_key = _iota((qb, 2 * qb), 1) + block_index * (2 * qb) >= qb
    dist_f = dist.astype(F32)
    lane_kv = _iota((qb, kv_w), 1) // HEAD_DIM
    n_blocks = rows // qb

    k_prev, v_prev = kprev_ref[...], vprev_ref[...]
    scores, values, sink_of = [], [], []
    for j in range(n_blocks):
        s = j * qb
        k_cur, v_cur = kn[s:s + qb], v[s:s + qb]
        kw = jnp.concatenate([k_prev, k_cur], axis=0)
        values.append(jnp.concatenate([v_prev, v_cur], axis=0))
        valid = in_window if j > 0 else in_window & has_key
        for kv in range(ATT_KV_HEADS):
            q_stack = jnp.concatenate(
                [jnp.where(lane_kv == kv, qn[s:s + qb, tile * kv_w:(tile + 1) * kv_w], 0.0) for tile in range(2)],
                axis=0)
            sc = _dot_nt(q_stack, kw) * HEAD_DIM ** -0.5
            for tile in range(2):
                head = kv * 2 + tile
                slope = 2.0 ** (-8.0 * (head + 1) / N_HEADS)
                scores.append(jnp.where(valid, sc[tile * qb:(tile + 1) * qb] - slope * dist_f, MASK_VALUE))
                sink_of.append(sink_logit(head))
        k_prev, v_prev = k_cur, v_cur
    work.step()
    kprev_ref[...] = k_prev
    vprev_ref[...] = v_prev

    maxes = [jnp.maximum(jnp.max(sc, axis=-1, keepdims=True), sink) for sc, sink in zip(scores, sink_of)]
    exps = [jnp.exp(sc - m) for sc, m in zip(scores, maxes)]
    work.step()
    probs = [e / (jnp.sum(e, axis=-1, keepdims=True) + jnp.exp(sink - m))
             for e, m, sink in zip(exps, maxes, sink_of)]

    o_parts = []
    for j in range(n_blocks):
        o_kv = [_dot(jnp.concatenate(probs[(j * ATT_KV_HEADS + kv) * 2:(j * ATT_KV_HEADS + kv + 1) * 2], axis=0),
                     values[j]) for kv in range(ATT_KV_HEADS)]
        o_tiles = [jnp.where(lane_kv == 0, o_kv[0][tile * qb:(tile + 1) * qb], o_kv[1][tile * qb:(tile + 1) * qb])
                   for tile in range(2)]
        o_parts.append(jnp.concatenate(o_tiles, axis=1))
        if j == 0:
            work.step()
    return jnp.concatenate(o_parts, axis=0)


def _rglru_block(xb, gb, conv_w, conv_b, wa, ba, wx, bx, lam, xpad_ref, h_ref, work):
    rows, width = xb.shape
    work.step()
    padded = jnp.concatenate([xpad_ref[...], xb], axis=0)
    xc = conv_b
    for j in range(CONV_WIDTH):
        back = CONV_WIDTH - 1 - j
        shifted = pltpu.roll(padded, back, axis=0) if back else padded
        xc = xc + shifted[SUBLANES:] * conv_w[j:j + 1, :]
    xpad_ref[...] = xb[rows - SUBLANES:rows, :]

    r_pre = _dot(xc, wa)
    i_pre = _dot(xc, wx)
    block_groups = ATT_BLOCK // SUBLANES
    sub = _iota((block_groups, SUBLANES, width), 1)
    scanned = []
    for s in range(0, rows, ATT_BLOCK):
        work.step()
        blk = slice(s, s + ATT_BLOCK)
        r = _sigmoid(r_pre[blk] + ba)
        i = _sigmoid(i_pre[blk] + bx)
        log_a = -LRU_C * r * _softplus(-lam)
        a = jnp.exp(log_a)
        u = jnp.sqrt(-jnp.tanh(log_a) * (a * a + 1.0)) * (i * xc[blk])
        a3 = a.reshape(block_groups, SUBLANES, width)
        u3 = u.reshape(block_groups, SUBLANES, width)
        shift = 1
        while shift < SUBLANES:
            a_prev = pltpu.roll(a3, shift, axis=1)
            u_prev = pltpu.roll(u3, shift, axis=1)
            take = sub >= shift
            u3 = jnp.where(take, a3 * u_prev + u3, u3)
            a3 = jnp.where(take, a3 * a_prev, a3)
            shift *= 2
        scanned.append((a3, u3))
    carry = h_ref[...]
    work.step(2)
    h_groups = []
    for a3, u3 in scanned:
        for grp in range(block_groups):
            h_grp = u3[grp] + a3[grp] * carry
            h_groups.append(h_grp)
            carry = h_grp[SUBLANES - 1:SUBLANES, :]
    h_ref[...] = carry
    return jnp.concatenate(h_groups, axis=0) * jax.nn.gelu(gb)


def _mixer_body(layer, x_ref, lbl_ref, sinks_ref, nw_ref, win_ref, waq_ref, gbias_ref, hgw_ref, gnw_ref, gnb_ref,
                qnw_ref, knw_ref, cw_ref, cb_ref, wa_ref, ba_ref, wx_ref, bx_ref, lam_ref,
                wbr_ref, wout_ref, o_ref,
                hg_state, ret_state, kprev, vprev, xpad, lru_h, gates, ys, h_s, pj_lru, pj_swa, pj_hgrn, pj_ret):
    t = pl.program_id(1)

    @pl.when(t == 0)
    def _():
        hg_state[...] = jnp.zeros_like(hg_state)
        ret_state[...] = jnp.zeros_like(ret_state)
        kprev[...] = jnp.zeros_like(kprev)
        vprev[...] = jnp.zeros_like(vprev)
        xpad[...] = jnp.zeros_like(xpad)
        lru_h[...] = jnp.zeros_like(lru_h)

    d = x_ref.shape[1]
    w = BRANCH_WIDTH
    kv_w = ATT_KV_HEADS * HEAD_DIM

    def proj(start, size):
        if (start, size) == (8 * w, w):
            weights = waq_ref[:, 0:w]
        else:
            assert start + size <= 8 * w or start >= 9 * w
            weights = win_ref[:, start:start + size]
        return jnp.dot(h_s[...], weights, preferred_element_type=F32)

    logits = lbl_ref[...]
    e = jnp.exp(logits - jnp.max(logits, axis=0, keepdims=True))
    p = e / jnp.sum(e, axis=0, keepdims=True)
    cum = p[0:1, :]
    for j in range(1, layer + 1):
        cum = cum + p[j:j + 1, :]
    lb = cum - p[0:1, :]

    names = ("hq", "hf", "hi", "hg", "rq", "rk", "rv", "rg", "aq", "ak", "av", "lx", "lg")
    sizes = (w, w, w, w, w, w, w, w, w, kv_w, kv_w, w, w)
    offsets = {}
    off = 0
    for name, size in zip(names, sizes):
        offsets[name] = (off, size)
        off += size
    gate_off = off

    pj_of = {"lx": pj_lru, "aq": pj_swa, "hq": pj_hgrn, "rq": pj_ret}

    def proj_piece(ref, base, start, size):
        ref[:, start - base:start - base + size] = proj(start, size)

    def proj_pieces(*group):
        start, stop = offsets[group[0]][0], sum(offsets[group[-1]])
        cols = w if group[0] == "aq" else PROJ_COLS
        return [functools.partial(proj_piece, pj_of[group[0]], start, s, min(cols, stop - s))
                for s in range(start, stop, cols)]

    def projected(*group):
        base = offsets[group[0]][0]
        return [pj_of[group[0]][:, offsets[name][0] - base:sum(offsets[name]) - base] for name in group]

    def gate_chunk(c):
        cols = slice(c * GATE_COLS, (c + 1) * GATE_COLS)
        gates[:, cols] = jnp.tanh(proj(gate_off + c * GATE_COLS, GATE_COLS) + gbias_ref[:, cols])

    swa_inputs, hgrn_inputs, ret_inputs = ("aq", "ak", "av"), ("hq", "hf", "hi", "hg"), ("rq", "rk", "rv", "rg")
    work = _WorkQueue()
    for group in (swa_inputs, hgrn_inputs, ret_inputs):
        for thunk in proj_pieces(*group):
            work.add(thunk)
    for c in range(N_BRANCH * d // GATE_COLS):
        work.add(functools.partial(gate_chunk, c))

    def lru_phase(work):
        h_s[...] = _rmsnorm_rows(x_ref[...], nw_ref[...]).astype(BF16)
        for thunk in proj_pieces("lx", "lg"):
            thunk()
        ys[3] = _rglru_block(*projected("lx", "lg"), cw_ref[...], cb_ref[...], wa_ref[:, 0:w],
                             ba_ref[...], wx_ref[:, 0:w], bx_ref[...], lam_ref[...], xpad, lru_h, work).astype(BF16)

    def swa_phase(work):
        ys[2] = _swa_block(*projected("aq", "ak", "av"),
                           qnw_ref[...], knw_ref[...], lambda head: sinks_ref[layer, head], kprev, vprev, t,
                           work).astype(BF16)

    def hgrn_phase(work):
        ys[0] = _hgrn2_block(*projected("hq", "hf", "hi", "hg"), lb, hgw_ref[...], hg_state, work).astype(BF16)

    def ret_phase(work):
        ys[1] = _retention_block(*projected("rq", "rk", "rv", "rg"), gnw_ref[...], gnb_ref[...], ret_state,
                                 work).astype(BF16)

    def merge_phase(work):
        work.flush()
        merged = jnp.zeros(x_ref.shape, F32)
        for n in range(N_BRANCH):
            z = jnp.dot(ys[n], wbr_ref[n, :, 0:d], preferred_element_type=F32)
            merged = merged + (gates[:, n * d:(n + 1) * d] * z + z)
        o_ref[...] = x_ref[...] + _dot(merged, wout_ref[:, 0:d])

    lru_phase(work)
    needed = 0
    for inputs, mixer_phase in ((swa_inputs, swa_phase), (hgrn_inputs, hgrn_phase), (ret_inputs, ret_phase)):
        needed += len(proj_pieces(*inputs))
        work.issue_until(needed)
        mixer_phase(work)
    merge_phase(work)


def _mixer(x2, n_seq, layer, lb_logits, sinks, per_layer):
    m, d = x2.shape
    seq = m // n_seq
    assert seq % MIX_ROWS == 0 and MIX_ROWS % ATT_BLOCK == 0
    steps = seq // MIX_ROWS
    w = BRANCH_WIDTH
    kv_w = ATT_KV_HEADS * HEAD_DIM
    operands = [x2, lb_logits, sinks] + list(per_layer)
    in_specs = [pl.BlockSpec((MIX_ROWS, d), lambda b, t: (b * steps + t, 0)),
                pl.BlockSpec(lb_logits.shape, lambda *_: (0, 0), pipeline_mode=pl.Buffered(1)),
                pl.BlockSpec(memory_space=pltpu.SMEM)]
    in_specs += [_layer_spec(op, layer) for op in per_layer]
    return pl.pallas_call(
        functools.partial(_mixer_body, layer),
        grid=(n_seq, steps),
        in_specs=in_specs,
        out_specs=pl.BlockSpec((MIX_ROWS, d), lambda b, t: (b * steps + t, 0)),
        out_shape=jax.ShapeDtypeStruct((m, d), F32),
        scratch_shapes=[
            pltpu.VMEM((HEAD_DIM, w), F32),
            pltpu.VMEM((w, w), F32),
            pltpu.VMEM((ATT_BLOCK, kv_w), F32),
            pltpu.VMEM((ATT_BLOCK, kv_w), F32),
            pltpu.VMEM((SUBLANES, w), F32),
            pltpu.VMEM((1, w), F32),
            pltpu.VMEM((MIX_ROWS, N_BRANCH * d), F32),
            pltpu.VMEM((N_BRANCH, MIX_ROWS, w), BF16),
            pltpu.VMEM((MIX_ROWS, d), BF16),
            pltpu.VMEM((MIX_ROWS, 2 * w), F32),
            pltpu.VMEM((MIX_ROWS, w + 2 * kv_w), F32),
            pltpu.VMEM((MIX_ROWS, 4 * w), F32),
            pltpu.VMEM((MIX_ROWS, 4 * w), F32),
        ],
        compiler_params=pltpu.CompilerParams(
            dimension_semantics=("arbitrary", "arbitrary"), vmem_limit_bytes=VMEM_LIMIT_BYTES),
        name="token_mixer",
    )(*operands)


def _block_diag(blocks):
    layers, n, c, e = blocks.shape
    eye = jnp.eye(n, dtype=blocks.dtype)
    return (eye[None, :, None, :, None] * blocks[:, :, :, None, :]).reshape(layers, n * c, n * e)


def _swa_head_order(t, axis):
    parts = jnp.split(t, N_HEADS, axis=axis)
    return jnp.concatenate([parts[0], parts[2], parts[1], parts[3]], axis=axis)


def kernel(x, ffn1_norm, ffn1_wg, ffn1_wu, ffn1_wd, mix_norm, w_in, gate_bias, hgrn_lb_logits, hgrn_out_norm, ret_gn_w, ret_gn_b, attn_q_norm, attn_k_norm, attn_sinks, lru_conv_w, lru_conv_b, lru_wa, lru_ba, lru_wx, lru_bx, lru_lambda, w_branch, w_out, ffn2_norm, ffn2_wg, ffn2_wu, ffn2_wd):
    n_seq, seq, d = x.shape
    depth = w_in.shape[0]
    w = BRANCH_WIDTH
    aq = 8 * w
    mw = _matmul_weight
    ffn1 = (ffn1_norm, mw(ffn1_wg), mw(ffn1_wu), mw(ffn1_wd))
    ffn2 = (ffn2_norm, mw(ffn2_wg), mw(ffn2_wu), mw(ffn2_wd))
    w_branch_perm = jnp.concatenate(
        [w_branch[:, :2], _swa_head_order(w_branch[:, 2], 1)[:, None], w_branch[:, 3:]], axis=1)
    gate_off = w_in.shape[2] - N_BRANCH * d
    col_scale = jnp.where(jnp.arange(w_in.shape[2]) >= gate_off, 0.5, 1.0).astype(w_in.dtype)
    mixer_operands = (
        _rows(mix_norm), mw(w_in * col_scale), mw(_swa_head_order(w_in[:, :, aq:aq + w], 2)),
        _rows(0.5 * gate_bias), _rows(hgrn_out_norm), _rows(ret_gn_w), _rows(ret_gn_b),
        _rows(jnp.tile(attn_q_norm, (1, N_HEADS))), _rows(jnp.tile(attn_k_norm, (1, ATT_KV_HEADS))),
        lru_conv_w, _rows(lru_conv_b), mw(_block_diag(lru_wa)), _rows(lru_ba),
        mw(_block_diag(lru_wx)), _rows(lru_bx), _rows(lru_lambda),
        mw(w_branch_perm), mw(0.5 * w_out))
    x2 = x.reshape(n_seq * seq, d)
    for l in range(depth):
        x2 = _ffn(x2, l, *ffn1)
        x2 = _mixer(x2, n_seq, l, hgrn_lb_logits, attn_sinks, mixer_operands)
        x2 = _ffn(x2, l, *ffn2)
    return x2.reshape(n_seq, seq, d)
```
